```python
import math
import jax, jax.numpy as jnp
from jax import lax
import numpy as np

D_MODEL = 2048
BATCH = 8
SEQ = 2048
DEPTH = 2

D_MIX = D_MODEL
A_HEADS = 4
A_HEAD_DIM = 128
A_QK_HALF = A_HEAD_DIM // 2
D_A = A_HEADS * A_HEAD_DIM
B_HEADS = 8
B_HEAD_DIM = 128
D_B = B_HEADS * B_HEAD_DIM
DILATED_PATTERNS = ((128, 1), (512, 4), (2048, 16))
D_C = D_MIX - D_A - D_B
CONV_WIDTH = 31

BLOCK = 128
N_BUCKETS = 32
MAX_DISTANCE = 2048
N_ATTN_HEADS = A_HEADS + B_HEADS
ALPHA = (2 * DEPTH) ** 0.25
BETA = (8 * DEPTH) ** -0.25
EPS = 1e-5

IN_SIZES = (D_A, D_A, D_A, D_A,
            D_B, D_B, D_B, D_B,
            D_C, D_C, D_C)
D_IN = sum(IN_SIZES)
IN_OFFSETS = [int(o) for o in np.cumsum(IN_SIZES)[:-1]]

kernel_name = "hymba_diff_dilated_conformer_deepnorm"


def layer_norm(x, g, b):
    x32 = x.astype(jnp.float32)
    mu = jnp.mean(x32, axis=-1, keepdims=True)
    var = jnp.mean(jnp.square(x32 - mu), axis=-1, keepdims=True)
    y = (x32 - mu) * lax.rsqrt(var + EPS) * g.astype(jnp.float32) + b.astype(jnp.float32)
    return y.astype(x.dtype)


def t5_bucket(dist):
    max_exact = N_BUCKETS // 2
    d = jnp.maximum(dist, 0)
    df = jnp.maximum(d, 1).astype(jnp.float32)
    large = max_exact + (jnp.log(df / max_exact) / math.log(MAX_DISTANCE / max_exact)
                         * (N_BUCKETS - max_exact)).astype(jnp.int32)
    large = jnp.minimum(large, N_BUCKETS - 1)
    return jnp.where(d < max_exact, d, large)


def diff_attention(q, k, v, bias_dist, lam, lam_init, head_gain):
    B, S, H, _, Dq = q.shape
    nb = S // BLOCK
    scale = Dq ** -0.5
    qb = jnp.moveaxis(q.reshape(B, nb, BLOCK, H, 2, Dq), 1, 0)
    k_pos = jnp.arange(S)

    def one_block(args):
        qi, i = args
        q_pos = i * BLOCK + jnp.arange(BLOCK)
        dist = q_pos[:, None] - k_pos[None, :]
        bias = bias_dist[jnp.clip(dist, 0, S - 1)].astype(jnp.float32)
        s = jnp.einsum('bqhcd,bkhcd->bhcqk', qi, k).astype(jnp.float32) * scale
        s = s + jnp.transpose(bias, (2, 0, 1))[None, :, None]
        s = jnp.where(dist >= 0, s, -jnp.inf)
        p = jax.nn.softmax(s, axis=-1)
        a = p[:, :, 0] - lam * p[:, :, 1]
        return jnp.einsum('bhqk,bkhd->bqhd', a.astype(v.dtype), v)

    out = lax.map(one_block, (qb, jnp.arange(nb)))
    out = jnp.moveaxis(out, 0, 1).reshape(B, S, H, v.shape[-1]).astype(jnp.float32)
    out = out * lax.rsqrt(jnp.mean(jnp.square(out), axis=-1, keepdims=True) + EPS)
    out = out * head_gain.astype(jnp.float32) * (1.0 - lam_init)
    return out.astype(v.dtype)


def _residue_view(x, dil):
    B, S = x.shape[:2]
    L = S // dil
    xr = jnp.moveaxis(x.reshape((B, L, dil) + x.shape[2:]), 2, 1).reshape((B * dil, L) + x.shape[2:])
    pad = (-L) % BLOCK
    return jnp.pad(xr, ((0, 0), (0, pad)) + ((0, 0),) * (x.ndim - 2))


def _residue_unview(y, B, dil, L):
    y = y[:, :L]
    y = jnp.moveaxis(y.reshape((B, dil) + y.shape[1:]), 1, 2)
    return y.reshape((B, dil * L) + y.shape[3:])


def dilated_attention(q, k, v, bias_table):
    B, S, H, D = q.shape
    scale = D ** -0.5
    qi = jnp.arange(BLOCK)
    kj = jnp.arange(2 * BLOCK)
    lag = BLOCK + qi[:, None] - kj[None, :]
    outs, maxes, sums = [], [], []
    for window, dil in DILATED_PATTERNS:
        L = S // dil
        qr, kr, vr = _residue_view(q, dil), _residue_view(k, dil), _residue_view(v, dil)
        N, Lp = qr.shape[:2]
        nb = Lp // BLOCK
        qb = qr.reshape(N, nb, BLOCK, H, D)
        kp = jnp.pad(kr, ((0, 0), (BLOCK, 0), (0, 0), (0, 0))).reshape(N, nb + 1, BLOCK, H, D)
        vp = jnp.pad(vr, ((0, 0), (BLOCK, 0), (0, 0), (0, 0))).reshape(N, nb + 1, BLOCK, H, D)
        kb = jnp.concatenate([kp[:, :-1], kp[:, 1:]], axis=2)
        vb = jnp.concatenate([vp[:, :-1], vp[:, 1:]], axis=2)
        key_idx = (jnp.arange(nb) * BLOCK - BLOCK)[:, None] + kj[None, :]
        valid = ((lag >= 0) & (lag <= window // dil))[None] & (key_idx >= 0)[:, None, :]
        bias = bias_table[t5_bucket(kj * dil)]
        bias_qk = bias[jnp.clip(lag, 0, 2 * BLOCK - 1)].astype(jnp.float32)
        s = jnp.einsum('nbqhd,nbkhd->nbhqk', qb, kb).astype(jnp.float32) * scale
        s = s + jnp.transpose(bias_qk, (2, 0, 1))[None, None]
        s = jnp.where(valid[None, :, None], s, -jnp.inf)
        m = jnp.max(s, axis=-1)
        p = jnp.exp(s - m[..., None])
        l = jnp.sum(p, axis=-1)
        o = jnp.einsum('nbhqk,nbkhd->nbqhd', p.astype(v.dtype), vb).astype(jnp.float32)
        l_t = jnp.swapaxes(l, 2, 3)
        o = o / l_t[..., None]
        outs.append(_residue_unview(o.reshape(N, Lp, H, D), B, dil, L))
        maxes.append(_residue_unview(jnp.swapaxes(m, 2, 3).reshape(N, Lp, H), B, dil, L))
        sums.append(_residue_unview(l_t.reshape(N, Lp, H), B, dil, L))
    m_all = jnp.stack(maxes)
    w = jnp.stack(sums) * jnp.exp(m_all - jnp.max(m_all, axis=0, keepdims=True))
    out = jnp.sum(w[..., None] * jnp.stack(outs), axis=0) / jnp.sum(w, axis=0)[..., None]
    return out.astype(q.dtype)


def conformer_conv(u, glu_gate, w_dw, b_dw, ln_g, ln_b, w_pw):
    h = u * jax.nn.sigmoid(glu_gate)
    h = lax.conv_general_dilated(h, w_dw[:, None, :], window_strides=(1,),
                                 padding=[(CONV_WIDTH - 1, 0)],
                                 dimension_numbers=('NWC', 'WIO', 'NWC'),
                                 feature_group_count=D_C) + b_dw
    h = jax.nn.silu(layer_norm(h, ln_g, ln_b))
    return h @ w_pw


def hybrid_layer(x, w_in, diff_lambda, diff_head_gain, conv_dw, conv_b, conv_ln_g, conv_ln_b,
                 conv_pw, w_out, ln_g, ln_b, rel_bias, lam_init):
    B, S, _ = x.shape
    proj = x @ w_in
    aq, ak, av, ag, bq, bk, bv, bg, cu, cglu, cg = jnp.split(proj, IN_OFFSETS, axis=-1)

    lam_v = diff_lambda.astype(jnp.float32)
    lam = (jnp.exp(jnp.sum(lam_v[0] * lam_v[1])) - jnp.exp(jnp.sum(lam_v[2] * lam_v[3]))
           + lam_init)
    bias_a = rel_bias[t5_bucket(jnp.arange(S))][:, :A_HEADS]
    ya = diff_attention(aq.reshape(B, S, A_HEADS, 2, A_QK_HALF),
                        ak.reshape(B, S, A_HEADS, 2, A_QK_HALF),
                        av.reshape(B, S, A_HEADS, A_HEAD_DIM),
                        bias_a, lam, lam_init, diff_head_gain).reshape(B, S, D_A)
    ya = ya * jax.nn.silu(ag)

    yb = dilated_attention(bq.reshape(B, S, B_HEADS, B_HEAD_DIM),
                           bk.reshape(B, S, B_HEADS, B_HEAD_DIM),
                           bv.reshape(B, S, B_HEADS, B_HEAD_DIM),
                           rel_bias[:, A_HEADS:]).reshape(B, S, D_B)
    yb = yb * jax.nn.silu(bg)

    yc = conformer_conv(cu, cglu, conv_dw, conv_b, conv_ln_g, conv_ln_b, conv_pw)
    yc = yc * jax.nn.silu(cg)

    y = jnp.concatenate([ya, yb, yc], axis=-1) @ w_out
    return layer_norm(ALPHA * x + y, ln_g, ln_b)


def setup_inputs(seed: int = 0) -> dict:
    key = jax.random.key(seed)
    ks = jax.random.split(key, 14)
    nrm = jax.random.normal
    return {
        "x": nrm(ks[0], (BATCH, SEQ, D_MODEL), jnp.float32),
        "w_in": nrm(ks[1], (DEPTH, D_MODEL, D_IN), jnp.float32) * D_MODEL ** -0.5,
        "diff_lambda": 0.1 * nrm(ks[2], (DEPTH, 4, A_QK_HALF), jnp.float32),
        "diff_head_gain": 1.0 + 0.02 * nrm(ks[3], (DEPTH, A_HEAD_DIM), jnp.float32),
        "conv_dw": nrm(ks[4], (DEPTH, CONV_WIDTH, D_C), jnp.float32) * CONV_WIDTH ** -0.5,
        "conv_b": 0.02 * nrm(ks[5], (DEPTH, D_C), jnp.float32),
        "conv_ln_g": 1.0 + 0.02 * nrm(ks[6], (DEPTH, D_C), jnp.float32),
        "conv_ln_b": 0.02 * nrm(ks[7], (DEPTH, D_C), jnp.float32),
        "conv_pw": nrm(ks[8], (DEPTH, D_C, D_C), jnp.float32) * D_C ** -0.5,
        "w_out": nrm(ks[9], (DEPTH, D_MIX, D_MODEL), jnp.float32) * (D_MIX ** -0.5 * BETA),
        "ln_g": 1.0 + 0.02 * nrm(ks[10], (DEPTH, D_MODEL), jnp.float32),
        "ln_b": 0.02 * nrm(ks[11], (DEPTH, D_MODEL), jnp.float32),
        "rel_bias": 0.5 * nrm(ks[12], (N_BUCKETS, N_ATTN_HEADS), jnp.float32),
    }


def reference(x, w_in, diff_lambda, diff_head_gain, conv_dw, conv_b, conv_ln_g, conv_ln_b,
              conv_pw, w_out, ln_g, ln_b, rel_bias):
    for layer in range(DEPTH):
        lam_init = 0.8 - 0.6 * math.exp(-0.3 * layer)
        x = hybrid_layer(x, w_in[layer], diff_lambda[layer], diff_head_gain[layer],
                         conv_dw[layer], conv_b[layer], conv_ln_g[layer], conv_ln_b[layer],
                         conv_pw[layer], w_out[layer], ln_g[layer], ln_b[layer], rel_bias,
                         lam_init)
    return x
```

```python
import functools
import math

import jax
import jax.numpy as jnp
from jax import lax
from jax.experimental import pallas as pl
from jax.experimental.pallas import tpu as pltpu

F32 = jnp.float32
BF16 = jnp.bfloat16

LANES = 128
D_MODEL = 2048
A_HEADS = 4
B_HEADS = 8
D_A = A_HEADS * LANES
D_B = B_HEADS * LANES
D_C = D_MODEL - D_A - D_B
C_BLOCKS = D_C // LANES
CONV_WIDTH = 31
CONV_PAD = 32
DILATIONS = (1, 4, 16)
BLOCK = 128
N_BUCKETS = 32
MAX_DISTANCE = 2048
EPS = 1e-5
D_IN = 4 * D_A + 4 * D_B + 3 * D_C
N_COL_BLOCKS = D_IN // LANES
COL_AQ, COL_AK, COL_AV, COL_AG = 0, 4, 8, 12
COL_BQ, COL_BK, COL_BV, COL_BG = 16, 24, 32, 40
COL_CU, COL_CGLU, COL_CG = 48, 52, 56

VMEM_LIMIT_BYTES = 56 * 1024 * 1024


def _params(semantics):
    return pltpu.CompilerParams(dimension_semantics=semantics, vmem_limit_bytes=VMEM_LIMIT_BYTES)


def _silu(x):
    return x * (1.0 / (1.0 + jnp.exp(-x)))


IN_TM = 1024
IN_TN = 1280


def _in_proj_kernel(x_ref, w_ref, o_ref, xb_ref):
    @pl.when(pl.program_id(1) == 0)
    def _():
        xb_ref[...] = x_ref[...].astype(BF16)

    acc = jnp.dot(xb_ref[...], w_ref[...], preferred_element_type=F32)
    for c in range(IN_TN // LANES):
        o_ref[c] = acc[:, c * LANES:(c + 1) * LANES].astype(BF16)


def _in_proj(x2d, w_bf16):
    m = x2d.shape[0]
    nb = IN_TN // LANES
    return pl.pallas_call(
        _in_proj_kernel,
        grid=(m // IN_TM, D_IN // IN_TN),
        in_specs=[
            pl.BlockSpec((IN_TM, D_MODEL), lambda i, j: (i, 0)),
            pl.BlockSpec((D_MODEL, IN_TN), lambda i, j: (0, j)),
        ],
        out_specs=pl.BlockSpec((nb, IN_TM, LANES), lambda i, j: (j, i, 0)),
        out_shape=jax.ShapeDtypeStruct((N_COL_BLOCKS, m, LANES), BF16),
        scratch_shapes=[pltpu.VMEM((IN_TM, D_MODEL), BF16)],
        compiler_params=_params(("parallel", "arbitrary")),
        name="in_proj",
    )(x2d, w_bf16)


def _t5_bucket(dist):
    max_exact = N_BUCKETS // 2
    d = jnp.maximum(dist, 0)
    df = jnp.maximum(d, 1).astype(F32)
    large = max_exact + (jnp.log(df / max_exact) / math.log(MAX_DISTANCE / max_exact)
                         * (N_BUCKETS - max_exact)).astype(jnp.int32)
    large = jnp.minimum(large, N_BUCKETS - 1)
    return jnp.where(d < max_exact, d, large)


DIFF_T = 256


def _diff_bias_tiles(rel_bias, seq):
    nd = seq // DIFF_T
    bias_dist = rel_bias[_t5_bucket(jnp.arange(seq))][:, :A_HEADS]
    r = jnp.arange(DIFF_T)[:, None]
    c = jnp.arange(DIFF_T)[None, :]
    dist = jnp.arange(nd)[:, None, None] * DIFF_T + r - c
    tiles = bias_dist[jnp.clip(dist, 0, seq - 1)].astype(F32)
    tiles = jnp.where(dist[..., None] >= 0, tiles, -jnp.inf)
    return jnp.transpose(tiles, (0, 3, 1, 2))


def _dil_bias_tiles(rel_bias):
    qi = jnp.arange(BLOCK)[:, None]
    kj = jnp.arange(2 * BLOCK)[None, :]
    lag = BLOCK + qi - kj
    valid = (lag >= 0) & (lag <= BLOCK)
    table = rel_bias[:, A_HEADS:]
    tiles = []
    for dil in DILATIONS:
        b = table[_t5_bucket(jnp.clip(lag, 0, 2 * BLOCK - 1) * dil)].astype(F32)
        b = jnp.where(valid[..., None], b, -jnp.inf)
        tiles.append(jnp.transpose(b, (2, 0, 1)))
    return jnp.stack(tiles)


def _diff_attn_kernel(lam_ref, q_ref, k_ref, v_ref, g_ref, bias_ref, gain_ref, o_ref, *,
                      scale, out_scale):
    i = pl.program_id(2)
    t = DIFF_T
    q = q_ref[...]
    lane = lax.broadcasted_iota(jnp.int32, q.shape, 1)
    zero = jnp.zeros_like(q)
    q2 = jnp.concatenate([jnp.where(lane < LANES // 2, q, zero),
                          jnp.where(lane >= LANES // 2, q, zero)], axis=0)

    def body(kj, carry):
        m, l, acc = carry
        off = pl.multiple_of(kj * t, t)
        k = k_ref[pl.ds(off, t), :]
        v = v_ref[pl.ds(off, t), :]
        b = bias_ref[i - kj]
        s = lax.dot_general(q2, k, (((1,), (1,)), ((), ())), preferred_element_type=F32) * scale
        s = s + jnp.concatenate([b, b], axis=0)
        m_new = jnp.maximum(m, jnp.max(s, axis=-1, keepdims=True))
        alpha = jnp.exp(m - m_new)
        p = jnp.exp(s - m_new)
        l = alpha * l + jnp.sum(p, axis=-1, keepdims=True)
        acc = alpha * acc + jnp.dot(p.astype(BF16), v, preferred_element_type=F32)
        return m_new, l, acc

    init = (jnp.full((2 * t, 1), -jnp.inf, F32), jnp.zeros((2 * t, 1), F32),
            jnp.zeros((2 * t, LANES), F32))
    _, l, acc = lax.fori_loop(0, i + 1, body, init)
    o = acc / l
    out = o[:t] - lam_ref[0] * o[t:]
    out = out * lax.rsqrt(jnp.mean(out * out, axis=-1, keepdims=True) + EPS)
    out = out * gain_ref[...] * out_scale
    o_ref[...] = (out * _silu(g_ref[...].astype(F32))).astype(BF16)


def _diff_attn(p4, bias, lam, gain, lam_init):
    _, b, s, _ = p4.shape
    nq = s // DIFF_T

    def col(first):
        return lambda bi, h, i: (first + h, bi, i, 0)

    def col_all(first):
        return lambda bi, h, i: (first + h, bi, 0, 0)

    kern = functools.partial(_diff_attn_kernel, scale=(LANES // 2) ** -0.5,
                             out_scale=1.0 - lam_init)
    return pl.pallas_call(
        kern,
        grid=(b, A_HEADS, nq),
        in_specs=[
            pl.BlockSpec(memory_space=pltpu.SMEM),
            pl.BlockSpec((None, None, DIFF_T, LANES), col(COL_AQ)),
            pl.BlockSpec((None, None, s, LANES), col_all(COL_AK)),
            pl.BlockSpec((None, None, s, LANES), col_all(COL_AV)),
            pl.BlockSpec((None, None, DIFF_T, LANES), col(COL_AG)),
            pl.BlockSpec((nq, None, DIFF_T, DIFF_T), lambda bi, h, i: (0, h, 0, 0)),
            pl.BlockSpec((1, LANES), lambda bi, h, i: (0, 0)),
        ],
        out_specs=pl.BlockSpec((None, DIFF_T, LANES), lambda bi, h, i: (bi, i, h)),
        out_shape=jax.ShapeDtypeStruct((b, s, D_A), BF16),
        compiler_params=_params(("parallel", "parallel", "arbitrary")),
        name="diff_attn",
    )(lam, p4, p4, p4, p4, bias, gain)


DIL_MERGE_ROWS = 256


def _dil_attn_kernel(q_ref, k_ref, v_ref, g_ref, bias_ref, o_ref,
                     qf, kf, vf, acc_s, m_s, l_s, *, seq, scale):
    qf[...] = q_ref[...].astype(F32)
    kf[...] = k_ref[...].astype(F32)
    vf[...] = v_ref[...].astype(F32)

    for p, dil in enumerate(DILATIONS):
        n_blocks = seq // dil // BLOCK
        bias = bias_ref[p]
        for r in range(dil):
            for j in range(n_blocks):
                def rows(jj, dil=dil, r=r):
                    start = dil * BLOCK * jj + r
                    if dil == 1:
                        return pl.ds(start, BLOCK)
                    return pl.ds(start, BLOCK, stride=dil)

                def take(ref16, ref32, jj, dil=dil):
                    if dil == 1:
                        return ref16[rows(jj), :]
                    return ref32[rows(jj), :].astype(BF16)

                qb = take(q_ref, qf, j)
                if j > 0:
                    kb = jnp.concatenate([take(k_ref, kf, j - 1), take(k_ref, kf, j)], axis=0)
                    vb = jnp.concatenate([take(v_ref, vf, j - 1), take(v_ref, vf, j)], axis=0)
                    b = bias
                else:
                    kb = take(k_ref, kf, j)
                    vb = take(v_ref, vf, j)
                    b = bias[:, BLOCK:]
                s = lax.dot_general(qb, kb, (((1,), (1,)), ((), ())),
                                    preferred_element_type=F32) * scale + b
                m = jnp.max(s, axis=-1, keepdims=True)
                e = jnp.exp(s - m)
                l = jnp.sum(e, axis=-1, keepdims=True)
                acc = jnp.dot(e.astype(BF16), vb, preferred_element_type=F32)
                acc_s[p, rows(j), :] = acc
                m_s[p, rows(j), :] = jnp.broadcast_to(m, (BLOCK, LANES))
                l_s[p, rows(j), :] = jnp.broadcast_to(l, (BLOCK, LANES))

    def merge(c, carry):
        r0 = pl.multiple_of(c * DIL_MERGE_ROWS, DIL_MERGE_ROWS)
        sl = pl.ds(r0, DIL_MERGE_ROWS)
        m0, m1, m2 = m_s[0, sl, :], m_s[1, sl, :], m_s[2, sl, :]
        m_all = jnp.maximum(jnp.maximum(m0, m1), m2)
        w0, w1, w2 = jnp.exp(m0 - m_all), jnp.exp(m1 - m_all), jnp.exp(m2 - m_all)
        num = w0 * acc_s[0, sl, :] + w1 * acc_s[1, sl, :] + w2 * acc_s[2, sl, :]
        den = w0 * l_s[0, sl, :] + w1 * l_s[1, sl, :] + w2 * l_s[2, sl, :]
        out = num / den
        o_ref[sl, :] = (out * _silu(g_ref[sl, :].astype(F32))).astype(BF16)
        return carry

    lax.fori_loop(0, seq // DIL_MERGE_ROWS, merge, 0)


def _dil_attn(p4, bias):
    _, b, s, _ = p4.shape
    n_pat = len(DILATIONS)

    def col(first):
        return lambda bi, h: (first + h, bi, 0, 0)

    head = pl.BlockSpec((None, None, s, LANES), col(COL_BQ))
    kern = functools.partial(_dil_attn_kernel, seq=s, scale=LANES ** -0.5)
    return pl.pallas_call(
        kern,
        grid=(b, B_HEADS),
        in_specs=[
            head,
            pl.BlockSpec((None, None, s, LANES), col(COL_BK)),
            pl.BlockSpec((None, None, s, LANES), col(COL_BV)),
            pl.BlockSpec((None, None, s, LANES), col(COL_BG)),
            pl.BlockSpec((n_pat, None, BLOCK, 2 * BLOCK), lambda bi, h: (0, h, 0, 0)),
        ],
        out_specs=pl.BlockSpec((None, s, LANES), lambda bi, h: (bi, 0, h)),
        out_shape=jax.ShapeDtypeStruct((b, s, D_B), BF16),
        scratch_shapes=[pltpu.VMEM((s, LANES), F32)] * 3
        + [pltpu.VMEM((n_pat, s, LANES), F32)] * 3,
        compiler_params=_params(("parallel", "parallel")),
        name="dil_attn",
    )(p4, p4, p4, p4, bias)


CONV_ROWS = 256


def _conv_kernel(u_ref, glu_ref, g_ref, dw_ref, db_ref, lng_ref, lnb_ref, pw_ref, o_ref,
                 hpad, hc, *, seq):
    for c in range(C_BLOCKS):
        hpad[c, pl.ds(0, CONV_PAD), :] = jnp.zeros((CONV_PAD, LANES), F32)
        u = u_ref[c].astype(F32)
        gate = glu_ref[c].astype(F32)
        hpad[c, pl.ds(CONV_PAD, seq), :] = u * (1.0 / (1.0 + jnp.exp(-gate)))

    first = CONV_PAD - (CONV_WIDTH - 1)
    for c in range(C_BLOCKS):
        cols = slice(c * LANES, (c + 1) * LANES)

        def conv_chunk(ci, carry, c=c, cols=cols):
            r0 = pl.multiple_of(ci * CONV_ROWS, CONV_ROWS)
            win = hpad[c, pl.ds(r0, CONV_ROWS + CONV_PAD), :]
            acc = jnp.broadcast_to(db_ref[:, cols], (CONV_ROWS, LANES))
            for j in range(CONV_WIDTH):
                acc = acc + dw_ref[pl.ds(j, 1), cols] * win[first + j:first + j + CONV_ROWS, :]
            hc[pl.ds(r0, CONV_ROWS), cols] = acc
            return carry

        lax.fori_loop(0, seq // CONV_ROWS, conv_chunk, 0)

    def tail(ci, carry):
        r0 = pl.multiple_of(ci * CONV_ROWS, CONV_ROWS)
        sl = pl.ds(r0, CONV_ROWS)
        x = hc[sl, :]
        mu = jnp.mean(x, axis=-1, keepdims=True)
        xc = x - mu
        var = jnp.mean(xc * xc, axis=-1, keepdims=True)
        y = xc * lax.rsqrt(var + EPS) * lng_ref[...] + lnb_ref[...]
        y = _silu(y)
        z = jnp.dot(y.astype(BF16), pw_ref[...], preferred_element_type=F32)
        for c in range(C_BLOCKS):
            cols = slice(c * LANES, (c + 1) * LANES)
            o_ref[sl, cols] = (z[:, cols] * _silu(g_ref[c, sl, :].astype(F32))).astype(BF16)
        return carry

    lax.fori_loop(0, seq // CONV_ROWS, tail, 0)


def _conv_mixer(p4, conv_dw, conv_b, ln_g, ln_b, pw_bf16):
    _, b, s, _ = p4.shape

    def cols(first):
        return lambda bi: (first // C_BLOCKS, bi, 0, 0)

    def whole(shape):
        return pl.BlockSpec(shape, lambda bi: (0,) * len(shape))

    blocks = (C_BLOCKS, None, s, LANES)
    return pl.pallas_call(
        functools.partial(_conv_kernel, seq=s),
        grid=(b,),
        in_specs=[
            pl.BlockSpec(blocks, cols(COL_CU)),
            pl.BlockSpec(blocks, cols(COL_CGLU)),
            pl.BlockSpec(blocks, cols(COL_CG)),
            whole((CONV_WIDTH, D_C)),
            whole((1, D_C)),
            whole((1, D_C)),
            whole((1, D_C)),
            whole((D_C, D_C)),
        ],
        out_specs=pl.BlockSpec((None, s, D_C), lambda bi: (bi, 0, 0)),
        out_shape=jax.ShapeDtypeStruct((b, s, D_C), BF16),
        scratch_shapes=[pltpu.VMEM((C_BLOCKS, CONV_PAD + s, LANES), F32),
                        pltpu.VMEM((s, D_C), F32)],
        compiler_params=_params(("parallel",)),
        name="conv_mixer",
    )(p4, p4, p4, conv_dw, conv_b.reshape(1, D_C), ln_g.reshape(1, D_C), ln_b.reshape(1, D_C),
      pw_bf16)


OUT_TM = 512


def _out_proj_kernel(ya_ref, yb_ref, yc_ref, x_ref, w_ref, g_ref, b_ref, o_ref, *, alpha):
    y = jnp.dot(ya_ref[...], w_ref[pl.ds(0, D_A), :], preferred_element_type=F32)
    y = y + jnp.dot(yb_ref[...], w_ref[pl.ds(D_A, D_B), :], preferred_element_type=F32)
    y = y + jnp.dot(yc_ref[...], w_ref[pl.ds(D_A + D_B, D_C), :], preferred_element_type=F32)
    z = alpha * x_ref[...] + y
    mu = jnp.mean(z, axis=-1, keepdims=True)
    zc = z - mu
    var = jnp.mean(zc * zc, axis=-1, keepdims=True)
    o_ref[...] = zc * lax.rsqrt(var + EPS) * g_ref[...] + b_ref[...]


def _out_proj(ya, yb, yc, x2d, w_bf16, ln_g, ln_b, alpha):
    m = x2d.shape[0]

    def rows(width):
        return pl.BlockSpec((OUT_TM, width), lambda i: (i, 0))

    def whole(shape):
        return pl.BlockSpec(shape, lambda i: (0, 0))

    return pl.pallas_call(
        functools.partial(_out_proj_kernel, alpha=alpha),
        grid=(m // OUT_TM,),
        in_specs=[rows(D_A), rows(D_B), rows(D_C), rows(D_MODEL),
                  whole((D_MODEL, D_MODEL)), whole((1, D_MODEL)), whole((1, D_MODEL))],
        out_specs=rows(D_MODEL),
        out_shape=jax.ShapeDtypeStruct((m, D_MODEL), F32),
        compiler_params=_params(("parallel",)),
        name="out_proj",
    )(ya, yb, yc, x2d, w_bf16, ln_g.reshape(1, D_MODEL), ln_b.reshape(1, D_MODEL))


def _layer(x2d, batch, seq, layer, depth, w_in, diff_lambda, diff_head_gain, conv_dw, conv_b,
           conv_ln_g, conv_ln_b, conv_pw, w_out, ln_g, ln_b, diff_bias, dil_bias):
    m = batch * seq
    lam_init = 0.8 - 0.6 * math.exp(-0.3 * layer)
    alpha = (2 * depth) ** 0.25
    proj = _in_proj(x2d, w_in.astype(BF16))
    p4 = proj.reshape(N_COL_BLOCKS, batch, seq, LANES)

    lam_v = diff_lambda.astype(F32)
    lam = (jnp.exp(jnp.sum(lam_v[0] * lam_v[1])) - jnp.exp(jnp.sum(lam_v[2] * lam_v[3]))
           + lam_init).reshape(1)
    ya = _diff_attn(p4, diff_bias, lam, diff_head_gain.astype(F32).reshape(1, LANES), lam_init)
    yb = _dil_attn(p4, dil_bias)
    yc = _conv_mixer(p4, conv_dw, conv_b, conv_ln_g, conv_ln_b, conv_pw.astype(BF16))
    return _out_proj(ya.reshape(m, D_A), yb.reshape(m, D_B), yc.reshape(m, D_C), x2d,
                     w_out.astype(BF16), ln_g, ln_b, alpha)


def kernel(x, w_in, diff_lambda, diff_head_gain, conv_dw, conv_b, conv_ln_g, conv_ln_b, conv_pw,
           w_out, ln_g, ln_b, rel_bias):
    batch, seq, _ = x.shape
    depth = w_in.shape[0]
    diff_bias = _diff_bias_tiles(rel_bias, seq)
    dil_bias = _dil_bias_tiles(rel_bias)
    x2d = x.reshape(batch * seq, D_MODEL)
    for layer in range(depth):
        x2d = _layer(x2d, batch, seq, layer, depth, w_in[layer], diff_lambda[layer],
                     diff_head_gain[layer], conv_dw[layer], conv_b[layer], conv_ln_g[layer],
                     conv_ln_b[layer], conv_pw[layer], w_out[layer], ln_g[layer], ln_b[layer],
                     diff_bias, dil_bias)
    return x2d.reshape(batch, seq, D_MODEL)
```

```python
import functools
import math

import jax
import jax.numpy as jnp
from jax import lax
from jax.experimental import pallas as pl
from jax.experimental.pallas import tpu as pltpu

F32 = jnp.float32
BF16 = jnp.bfloat16

LANES = 128
D_MODEL = 2048
A_HEADS = 4
B_HEADS = 8
D_A = A_HEADS * LANES
D_B = B_HEADS * LANES
D_C = D_MODEL - D_A - D_B
C_BLOCKS = D_C // LANES
CONV_WIDTH = 31
CONV_PAD = 32
DILATIONS = (1, 4, 16)
BLOCK = 128
N_BUCKETS = 32
MAX_DISTANCE = 2048
EPS = 1e-5
D_IN = 4 * D_A + 4 * D_B + 3 * D_C
N_COL_BLOCKS = D_IN // LANES
COL_AQ, COL_AK, COL_AV, COL_AG = 0, 4, 8, 12
COL_BQ, COL_BK, COL_BV, COL_BG = 16, 24, 32, 40
COL_CU, COL_CGLU, COL_CG = 48, 52, 56

VMEM_LIMIT_BYTES = 56 * 1024 * 1024


def _params(semantics):
    return pltpu.CompilerParams(dimension_semantics=semantics, vmem_limit_bytes=VMEM_LIMIT_BYTES)


def _silu(x):
    return x * (1.0 / (1.0 + jnp.exp(-x)))


IN_TM = 1024
IN_TN = 1280


def _in_proj_kernel(x_ref, w_ref, o_ref, xb_ref):
    @pl.when(pl.program_id(1) == 0)
    def _():
        xb_ref[...] = x_ref[...].astype(BF16)

    acc = jnp.dot(xb_ref[...], w_ref[...], preferred_element_type=F32)
    for c in range(IN_TN // LANES):
        o_ref[c] = acc[:, c * LANES:(c + 1) * LANES].astype(BF16)


def _in_proj(x2d, w_bf16):
    m = x2d.shape[0]
    nb = IN_TN // LANES
    return pl.pallas_call(
        _in_proj_kernel,
        grid=(m // IN_TM, D_IN // IN_TN),
        in_specs=[
            pl.BlockSpec((IN_TM, D_MODEL), lambda i, j: (i, 0)),
            pl.BlockSpec((D_MODEL, IN_TN), lambda i, j: (0, j)),
        ],
        out_specs=pl.BlockSpec((nb, IN_TM, LANES), lambda i, j: (j, i, 0)),
        out_shape=jax.ShapeDtypeStruct((N_COL_BLOCKS, m, LANES), BF16),
        scratch_shapes=[pltpu.VMEM((IN_TM, D_MODEL), BF16)],
        compiler_params=_params(("parallel", "arbitrary")),
        name="in_proj",
    )(x2d, w_bf16)


def _t5_bucket(dist):
    max_exact = N_BUCKETS // 2
    d = jnp.maximum(dist, 0)
    df = jnp.maximum(d, 1).astype(F32)
    large = max_exact + (jnp.log(df / max_exact) / math.log(MAX_DISTANCE / max_exact)
                         * (N_BUCKETS - max_exact)).astype(jnp.int32)
    large = jnp.minimum(large, N_BUCKETS - 1)
    return jnp.where(d < max_exact, d, large)


DIFF_T = 256


def _bucket_lookup(table, bucket):
    onehot = bucket[..., None] == jnp.arange(N_BUCKETS)
    cols = [jnp.sum(jnp.where(onehot, table[:, h].astype(F32), 0.0), axis=-1)
            for h in range(table.shape[1])]
    return jnp.stack(cols)


def _diff_bias_tiles(rel_bias, seq):
    nd = seq // DIFF_T
    r = jnp.arange(DIFF_T)[:, None]
    c = jnp.arange(DIFF_T)[None, :]
    dist = jnp.arange(nd)[:, None, None] * DIFF_T + r - c
    tiles = _bucket_lookup(rel_bias[:, :A_HEADS], _t5_bucket(jnp.clip(dist, 0, seq - 1)))
    tiles = jnp.where(dist[None] >= 0, tiles, -jnp.inf)
    return jnp.transpose(tiles, (1, 0, 2, 3))


def _dil_bias_tiles(rel_bias):
    qi = jnp.arange(BLOCK)[:, None]
    kj = jnp.arange(2 * BLOCK)[None, :]
    lag = BLOCK + qi - kj
    valid = (lag >= 0) & (lag <= BLOCK)
    tiles = []
    for dil in DILATIONS:
        b = _bucket_lookup(rel_bias[:, A_HEADS:],
                           _t5_bucket(jnp.clip(lag, 0, 2 * BLOCK - 1) * dil))
        tiles.append(jnp.where(valid[None], b, -jnp.inf))
    return jnp.stack(tiles)


def _diff_attn_kernel(lam_ref, q_ref, k_ref, v_ref, g_ref, bias_ref, gain_ref, o_ref, *,
                      scale, out_scale):
    i = pl.program_id(2)
    t = DIFF_T
    q = q_ref[...]
    lane = lax.broadcasted_iota(jnp.int32, q.shape, 1)
    zero = jnp.zeros_like(q)
    q2 = jnp.concatenate([jnp.where(lane < LANES // 2, q, zero),
                          jnp.where(lane >= LANES // 2, q, zero)], axis=0)

    def body(kj, carry):
        m, l, acc = carry
        off = pl.multiple_of(kj * t, t)
        k = k_ref[pl.ds(off, t), :]
        v = v_ref[pl.ds(off, t), :]
        b = bias_ref[i - kj]
        s = lax.dot_general(q2, k, (((1,), (1,)), ((), ())), preferred_element_type=F32) * scale
        s = s + jnp.concatenate([b, b], axis=0)
        m_new = jnp.maximum(m, jnp.max(s, axis=-1, keepdims=True))
        alpha = jnp.exp(m - m_new)
        p = jnp.exp(s - m_new)
        l = alpha * l + jnp.sum(p, axis=-1, keepdims=True)
        acc = alpha * acc + jnp.dot(p.astype(BF16), v, preferred_element_type=F32)
        return m_new, l, acc

    init = (jnp.full((2 * t, 1), -jnp.inf, F32), jnp.zeros((2 * t, 1), F32),
            jnp.zeros((2 * t, LANES), F32))
    _, l, acc = lax.fori_loop(0, i + 1, body, init)
    o = acc / l
    out = o[:t] - lam_ref[0] * o[t:]
    out = out * lax.rsqrt(jnp.mean(out * out, axis=-1, keepdims=True) + EPS)
    out = out * gain_ref[...] * out_scale
    o_ref[...] = (out * _silu(g_ref[...].astype(F32))).astype(BF16)


def _diff_attn(p4, bias, lam, gain, lam_init):
    _, b, s, _ = p4.shape
    nq = s // DIFF_T

    def col(first):
        return lambda bi, h, i: (first + h, bi, i, 0)

    def col_all(first):
        return lambda bi, h, i: (first + h, bi, 0, 0)

    kern = functools.partial(_diff_attn_kernel, scale=(LANES // 2) ** -0.5,
                             out_scale=1.0 - lam_init)
    return pl.pallas_call(
        kern,
        grid=(b, A_HEADS, nq),
        in_specs=[
            pl.BlockSpec(memory_space=pltpu.SMEM),
            pl.BlockSpec((None, None, DIFF_T, LANES), col(COL_AQ)),
            pl.BlockSpec((None, None, s, LANES), col_all(COL_AK)),
            pl.BlockSpec((None, None, s, LANES), col_all(COL_AV)),
            pl.BlockSpec((None, None, DIFF_T, LANES), col(COL_AG)),
            pl.BlockSpec((nq, None, DIFF_T, DIFF_T), lambda bi, h, i: (0, h, 0, 0)),
            pl.BlockSpec((1, LANES), lambda bi, h, i: (0, 0)),
        ],
        out_specs=pl.BlockSpec((None, DIFF_T, LANES), lambda bi, h, i: (bi, i, h)),
        out_shape=jax.ShapeDtypeStruct((b, s, D_A), BF16),
        compiler_params=_params(("parallel", "parallel", "arbitrary")),
        name="diff_attn",
    )(lam, p4, p4, p4, p4, bias, gain)


DIL_MERGE_ROWS = 256


def _dil_attn_kernel(q_ref, k_ref, v_ref, g_ref, bias_ref, o_ref,
                     qf, kf, vf, acc_s, m_s, l_s, *, seq, scale):
    qf[...] = q_ref[...].astype(F32)
    kf[...] = k_ref[...].astype(F32)
    vf[...] = v_ref[...].astype(F32)

    for p, dil in enumerate(DILATIONS):
        n_blocks = seq // dil // BLOCK
        bias = bias_ref[p]
        for r in range(dil):
            for j in range(n_blocks):
                def rows(jj, dil=dil, r=r):
                    start = dil * BLOCK * jj + r
                    if dil == 1:
                        return pl.ds(start, BLOCK)
                    return pl.ds(start, BLOCK, stride=dil)

                def take(ref16, ref32, jj, dil=dil):
                    if dil == 1:
                        return ref16[rows(jj), :]
                    return ref32[rows(jj), :].astype(BF16)

                qb = take(q_ref, qf, j)
                if j > 0:
                    kb = jnp.concatenate([take(k_ref, kf, j - 1), take(k_ref, kf, j)], axis=0)
                    vb = jnp.concatenate([take(v_ref, vf, j - 1), take(v_ref, vf, j)], axis=0)
                    b = bias
                else:
                    kb = take(k_ref, kf, j)
                    vb = take(v_ref, vf, j)
                    b = bias[:, BLOCK:]
                s = lax.dot_general(qb, kb, (((1,), (1,)), ((), ())),
                                    preferred_element_type=F32) * scale + b
                m = jnp.max(s, axis=-1, keepdims=True)
                e = jnp.exp(s - m)
                l = jnp.sum(e, axis=-1, keepdims=True)
                acc = jnp.dot(e.astype(BF16), vb, preferred_element_type=F32)
                acc_s[p, rows(j), :] = acc
                m_s[p, rows(j), :] = jnp.broadcast_to(m, (BLOCK, LANES))
                l_s[p, rows(j), :] = jnp.broadcast_to(l, (BLOCK, LANES))

    def merge(c, carry):
        r0 = pl.multiple_of(c * DIL_MERGE_ROWS, DIL_MERGE_ROWS)
        sl = pl.ds(r0, DIL_MERGE_ROWS)
        m0, m1, m2 = m_s[0, sl, :], m_s[1, sl, :], m_s[2, sl, :]
        m_all = jnp.maximum(jnp.maximum(m0, m1), m2)
        w0, w1, w2 = jnp.exp(m0 - m_all), jnp.exp(m1 - m_all), jnp.exp(m2 - m_all)
        num = w0 * acc_s[0, sl, :] + w1 * acc_s[1, sl, :] + w2 * acc_s[2, sl, :]
        den = w0 * l_s[0, sl, :] + w1 * l_s[1, sl, :] + w2 * l_s[2, sl, :]
        out = num / den
        o_ref[sl, :] = (out * _silu(g_ref[sl, :].astype(F32))).astype(BF16)
        return carry

    lax.fori_loop(0, seq // DIL_MERGE_ROWS, merge, 0)


def _dil_attn(p4, bias):
    _, b, s, _ = p4.shape
    n_pat = len(DILATIONS)

    def col(first):
        return lambda bi, h: (first + h, bi, 0, 0)

    head = pl.BlockSpec((None, None, s, LANES), col(COL_BQ))
    kern = functools.partial(_dil_attn_kernel, seq=s, scale=LANES ** -0.5)
    return pl.pallas_call(
        kern,
        grid=(b, B_HEADS),
        in_specs=[
            head,
            pl.BlockSpec((None, None, s, LANES), col(COL_BK)),
            pl.BlockSpec((None, None, s, LANES), col(COL_BV)),
            pl.BlockSpec((None, None, s, LANES), col(COL_BG)),
            pl.BlockSpec((n_pat, None, BLOCK, 2 * BLOCK), lambda bi, h: (0, h, 0, 0)),
        ],
        out_specs=pl.BlockSpec((None, s, LANES), lambda bi, h: (bi, 0, h)),
        out_shape=jax.ShapeDtypeStruct((b, s, D_B), BF16),
        scratch_shapes=[pltpu.VMEM((s, LANES), F32)] * 3
        + [pltpu.VMEM((n_pat, s, LANES), F32)] * 3,
        compiler_params=_params(("parallel", "parallel")),
        name="dil_attn",
    )(p4, p4, p4, p4, bias)


CONV_ROWS = 256


def _conv_kernel(u_ref, glu_ref, g_ref, dw_ref, db_ref, lng_ref, lnb_ref, pw_ref, o_ref,
                 hpad, hc, *, seq):
    for c in range(C_BLOCKS):
        hpad[c, pl.ds(0, CONV_PAD), :] = jnp.zeros((CONV_PAD, LANES), F32)
        u = u_ref[c].astype(F32)
        gate = glu_ref[c].astype(F32)
        hpad[c, pl.ds(CONV_PAD, seq), :] = u * (1.0 / (1.0 + jnp.exp(-gate)))

    first = CONV_PAD - (CONV_WIDTH - 1)
    for c in range(C_BLOCKS):
        cols = slice(c * LANES, (c + 1) * LANES)

        def conv_chunk(ci, carry, c=c, cols=cols):
            r0 = pl.multiple_of(ci * CONV_ROWS, CONV_ROWS)
            win = hpad[c, pl.ds(r0, CONV_ROWS + CONV_PAD), :]
            acc = jnp.broadcast_to(db_ref[:, cols], (CONV_ROWS, LANES))
            for j in range(CONV_WIDTH):
                acc = acc + dw_ref[pl.ds(j, 1), cols] * win[first + j:first + j + CONV_ROWS, :]
            hc[pl.ds(r0, CONV_ROWS), cols] = acc
            return carry

        lax.fori_loop(0, seq // CONV_ROWS, conv_chunk, 0)

    def tail(ci, carry):
        r0 = pl.multiple_of(ci * CONV_ROWS, CONV_ROWS)
        sl = pl.ds(r0, CONV_ROWS)
        x = hc[sl, :]
        mu = jnp.mean(x, axis=-1, keepdims=True)
        xc = x - mu
        var = jnp.mean(xc * xc, axis=-1, keepdims=True)
        y = xc * lax.rsqrt(var + EPS) * lng_ref[...] + lnb_ref[...]
        y = _silu(y)
        z = jnp.dot(y.astype(BF16), pw_ref[...], preferred_element_type=F32)
        for c in range(C_BLOCKS):
            cols = slice(c * LANES, (c + 1) * LANES)
            o_ref[sl, cols] = (z[:, cols] * _silu(g_ref[c, sl, :].astype(F32))).astype(BF16)
        return carry

    lax.fori_loop(0, seq // CONV_ROWS, tail, 0)


def _conv_mixer(p4, conv_dw, conv_b, ln_g, ln_b, pw_bf16):
    _, b, s, _ = p4.shape

    def cols(first):
        return lambda bi: (first // C_BLOCKS, bi, 0, 0)

    def whole(shape):
        return pl.BlockSpec(shape, lambda bi: (0,) * len(shape))

    blocks = (C_BLOCKS, None, s, LANES)
    return pl.pallas_call(
        functools.partial(_conv_kernel, seq=s),
        grid=(b,),
        in_specs=[
            pl.BlockSpec(blocks, cols(COL_CU)),
            pl.BlockSpec(blocks, cols(COL_CGLU)),
            pl.BlockSpec(blocks, cols(COL_CG)),
            whole((CONV_WIDTH, D_C)),
            whole((1, D_C)),
            whole((1, D_C)),
            whole((1, D_C)),
            whole((D_C, D_C)),
        ],
        out_specs=pl.BlockSpec((None, s, D_C), lambda bi: (bi, 0, 0)),
        out_shape=jax.ShapeDtypeStruct((b, s, D_C), BF16),
        scratch_shapes=[pltpu.VMEM((C_BLOCKS, CONV_PAD + s, LANES), F32),
                        pltpu.VMEM((s, D_C), F32)],
        compiler_params=_params(("parallel",)),
        name="conv_mixer",
    )(p4, p4, p4, conv_dw, conv_b.reshape(1, D_C), ln_g.reshape(1, D_C), ln_b.reshape(1, D_C),
      pw_bf16)


OUT_TM = 512


def _out_proj_kernel(ya_ref, yb_ref, yc_ref, x_ref, w_ref, g_ref, b_ref, o_ref, *, alpha):
    y = jnp.dot(ya_ref[...], w_ref[pl.ds(0, D_A), :], preferred_element_type=F32)
    y = y + jnp.dot(yb_ref[...], w_ref[pl.ds(D_A, D_B), :], preferred_element_type=F32)
    y = y + jnp.dot(yc_ref[...], w_ref[pl.ds(D_A + D_B, D_C), :], preferred_element_type=F32)
    z = alpha * x_ref[...] + y
    mu = jnp.mean(z, axis=-1, keepdims=True)
    zc = z - mu
    var = jnp.mean(zc * zc, axis=-1, keepdims=True)
    o_ref[...] = zc * lax.rsqrt(var + EPS) * g_ref[...] + b_ref[...]


def _out_proj(ya, yb, yc, x2d, w_bf16, ln_g, ln_b, alpha):
    m = x2d.shape[0]

    def rows(width):
        return pl.BlockSpec((OUT_TM, width), lambda i: (i, 0))

    def whole(shape):
        return pl.BlockSpec(shape, lambda i: (0, 0))

    return pl.pallas_call(
        functools.partial(_out_proj_kernel, alpha=alpha),
        grid=(m // OUT_TM,),
        in_specs=[rows(D_A), rows(D_B), rows(D_C), rows(D_MODEL),
                  whole((D_MODEL, D_MODEL)), whole((1, D_MODEL)), whole((1, D_MODEL))],
        out_specs=rows(D_MODEL),
        out_shape=jax.ShapeDtypeStruct((m, D_MODEL), F32),
        compiler_params=_params(("parallel",)),
        name="out_proj",
    )(ya, yb, yc, x2d, w_bf16, ln_g.reshape(1, D_MODEL), ln_b.reshape(1, D_MODEL))


def _layer(x2d, batch, seq, layer, depth, w_in, diff_lambda, diff_head_gain, conv_dw, conv_b,
           conv_ln_g, conv_ln_b, conv_pw, w_out, ln_g, ln_b, diff_bias, dil_bias):
    m = batch * seq
    lam_init = 0.8 - 0.6 * math.exp(-0.3 * layer)
    alpha = (2 * depth) ** 0.25
    proj = _in_proj(x2d, w_in.astype(BF16))
    p4 = proj.reshape(N_COL_BLOCKS, batch, seq, LANES)

    lam_v = diff_lambda.astype(F32)
    lam = (jnp.exp(jnp.sum(lam_v[0] * lam_v[1])) - jnp.exp(jnp.sum(lam_v[2] * lam_v[3]))
           + lam_init).reshape(1)
    ya = _diff_attn(p4, diff_bias, lam, diff_head_gain.astype(F32).reshape(1, LANES), lam_init)
    yb = _dil_attn(p4, dil_bias)
    yc = _conv_mixer(p4, conv_dw, conv_b, conv_ln_g, conv_ln_b, conv_pw.astype(BF16))
    return _out_proj(ya.reshape(m, D_A), yb.reshape(m, D_B), yc.reshape(m, D_C), x2d,
                     w_out.astype(BF16), ln_g, ln_b, alpha)


def kernel(x, w_in, diff_lambda, diff_head_gain, conv_dw, conv_b, conv_ln_g, conv_ln_b, conv_pw,
           w_out, ln_g, ln_b, rel_bias):
    batch, seq, _ = x.shape
    depth = w_in.shape[0]
    diff_bias = _diff_bias_tiles(rel_bias, seq)
    dil_bias = _dil_bias_tiles(rel_bias)
    x2d = x.reshape(batch * seq, D_MODEL)
    for layer in range(depth):
        x2d = _layer(x2d, batch, seq, layer, depth, w_in[layer], diff_lambda[layer],
                     diff_head_gain[layer], conv_dw[layer], conv_b[layer], conv_ln_g[layer],
                     conv_ln_b[layer], conv_pw[layer], w_out[layer], ln_g[layer], ln_b[layer],
                     diff_bias, dil_bias)
    return x2d.reshape(batch, seq, D_MODEL)
```

```python
import functools
import math

import jax
import jax.numpy as jnp
from jax import lax
from jax.experimental import pallas as pl
from jax.experimental.pallas import tpu as pltpu

F32 = jnp.float32
BF16 = jnp.bfloat16

LANES = 128
D_MODEL = 2048
A_HEADS = 4
B_HEADS = 8
D_A = A_HEADS * LANES
D_B = B_HEADS * LANES
D_C = D_MODEL - D_A - D_B
C_BLOCKS = D_C // LANES
CONV_WIDTH = 31
CONV_PAD = 32
DILATIONS = (1, 4, 16)
BLOCK = 128
N_BUCKETS = 32
MAX_DISTANCE = 2048
EPS = 1e-5
LOG2E = math.log2(math.e)
D_IN = 4 * D_A + 4 * D_B + 3 * D_C
N_COL_BLOCKS = D_IN // LANES
COL_AQ, COL_AK, COL_AV, COL_AG = 0, 4, 8, 12
COL_BQ, COL_BK, COL_BV, COL_BG = 16, 24, 32, 40
COL_CU, COL_CGLU, COL_CG = 48, 52, 56

VMEM_LIMIT_BYTES = 56 * 1024 * 1024


def _params(semantics):
    return pltpu.CompilerParams(dimension_semantics=semantics, vmem_limit_bytes=VMEM_LIMIT_BYTES)


def _silu(x):
    return x * (1.0 / (1.0 + jnp.exp(-x)))


IN_TM = 1024
IN_TN = 1280


def _in_proj_kernel(x_ref, w_ref, o_ref, xb_ref):
    @pl.when(pl.program_id(1) == 0)
    def _():
        xb_ref[...] = x_ref[...].astype(BF16)

    acc = jnp.dot(xb_ref[...], w_ref[...], preferred_element_type=F32)
    for c in range(IN_TN // LANES):
        o_ref[c] = acc[:, c * LANES:(c + 1) * LANES].astype(BF16)


def _in_proj(x2d, w_bf16):
    m = x2d.shape[0]
    nb = IN_TN // LANES
    return pl.pallas_call(
        _in_proj_kernel,
        grid=(m // IN_TM, D_IN // IN_TN),
        in_specs=[
            pl.BlockSpec((IN_TM, D_MODEL), lambda i, j: (i, 0)),
            pl.BlockSpec((D_MODEL, IN_TN), lambda i, j: (0, j)),
        ],
        out_specs=pl.BlockSpec((nb, IN_TM, LANES), lambda i, j: (j, i, 0)),
        out_shape=jax.ShapeDtypeStruct((N_COL_BLOCKS, m, LANES), BF16),
        scratch_shapes=[pltpu.VMEM((IN_TM, D_MODEL), BF16)],
        compiler_params=_params(("parallel", "arbitrary")),
        name="in_proj",
    )(x2d, w_bf16)


def _t5_bucket(dist):
    max_exact = N_BUCKETS // 2
    d = jnp.maximum(dist, 0)
    df = jnp.maximum(d, 1).astype(F32)
    large = max_exact + (jnp.log(df / max_exact) / math.log(MAX_DISTANCE / max_exact)
                         * (N_BUCKETS - max_exact)).astype(jnp.int32)
    large = jnp.minimum(large, N_BUCKETS - 1)
    return jnp.where(d < max_exact, d, large)


DIFF_T = 256


def _bucket_lookup(table, bucket):
    onehot = bucket[..., None] == jnp.arange(N_BUCKETS)
    cols = [jnp.sum(jnp.where(onehot, table[:, h].astype(F32), 0.0), axis=-1)
            for h in range(table.shape[1])]
    return jnp.stack(cols)


def _diff_bias_tiles(rel_bias, seq):
    nd = seq // DIFF_T
    key = jnp.arange(DIFF_T)[:, None]
    qry = jnp.arange(DIFF_T)[None, :]
    dist = jnp.arange(nd)[:, None, None] * DIFF_T + qry - key
    tiles = _bucket_lookup(rel_bias[:, :A_HEADS], _t5_bucket(jnp.clip(dist, 0, seq - 1)))
    tiles = jnp.where(dist[None] >= 0, tiles * LOG2E, -jnp.inf)
    return jnp.transpose(tiles, (1, 0, 2, 3))


def _dil_bias_tiles(rel_bias):
    qi = jnp.arange(BLOCK)[:, None]
    kj = jnp.arange(2 * BLOCK)[None, :]
    lag = BLOCK + qi - kj
    valid = (lag >= 0) & (lag <= BLOCK)
    tiles = []
    for dil in DILATIONS:
        b = _bucket_lookup(rel_bias[:, A_HEADS:],
                           _t5_bucket(jnp.clip(lag, 0, 2 * BLOCK - 1) * dil))
        tiles.append(jnp.where(valid[None], b, -jnp.inf))
    return jnp.stack(tiles)


DIFF_UNROLL = 2
ONES_ROWS = 16


def _diff_pairs(nq):
    return [(qi, kj) for qi in range(nq) for kj in range(qi + 1)]


def _diff_attn_kernel(lam_ref, qi_ref, kj_ref, q_ref, k_ref, v_ref, g_ref, bias_ref, gain_ref,
                      o_ref, vt_s, q2t_s, acc_s, m_s, *, out_scale):
    t = DIFF_T
    nq = q_ref.shape[0] // t
    n_pairs = len(_diff_pairs(nq))
    u = DIFF_UNROLL
    assert n_pairs % u == 0 and n_pairs >= 2 * u

    half = lax.broadcasted_iota(jnp.int32, (LANES, t), 0) < LANES // 2
    ones = jnp.ones((ONES_ROWS, t), BF16)
    for j in range(nq):
        rows = pl.ds(j * t, t)
        vt_s[j] = jnp.concatenate([v_ref[rows, :].astype(F32).T.astype(BF16), ones], axis=0)
        qt = q_ref[rows, :].astype(F32).T
        q2t_s[j] = jnp.concatenate([jnp.where(half, qt, 0.0), jnp.where(half, 0.0, qt)],
                                   axis=1).astype(BF16)

    def probs(n):
        qi, kj = qi_ref[n], kj_ref[n]
        k = k_ref[pl.ds(pl.multiple_of(kj * t, t), t), :]
        b = bias_ref[qi - kj]
        s = jnp.dot(k, q2t_s[qi], preferred_element_type=F32)
        s = s + jnp.concatenate([b, b], axis=1)
        m_blk = jnp.max(s, axis=0, keepdims=True)
        m_s[n] = m_blk
        return jnp.exp2(s - m_blk).astype(BF16)

    def values(n, p):
        acc_s[n] = jnp.dot(vt_s[kj_ref[n]], p, preferred_element_type=F32)

    def body(it, carry):
        n = it * u
        nxt = tuple(probs(n + u + c) for c in range(u))
        for c in range(u):
            values(n + c, carry[c])
        return nxt

    carry = lax.fori_loop(0, n_pairs // u - 1, body, tuple(probs(c) for c in range(u)))
    for c in range(u):
        values(n_pairs - u + c, carry[c])

    lam = lam_ref[0]
    for qi in range(nq):
        first = qi * (qi + 1) // 2
        ms = [m_s[first + j] for j in range(qi + 1)]
        m_all = functools.reduce(jnp.maximum, ms)
        acc = jnp.zeros((LANES + ONES_ROWS, 2 * t), F32)
        for j in range(qi + 1):
            acc = acc + jnp.exp2(ms[j] - m_all) * acc_s[first + j]
        o = acc[:LANES] / acc[LANES:LANES + 1]
        out = o[:, :t] - lam * o[:, t:]
        out = out * lax.rsqrt(jnp.mean(out * out, axis=0, keepdims=True) + EPS)
        out = (out * gain_ref[...] * out_scale).T
        rows = pl.ds(qi * t, t)
        o_ref[rows, :] = (out * _silu(g_ref[rows, :].astype(F32))).astype(BF16)


def _diff_attn(p4, bias, lam, gain, lam_init):
    _, b, s, _ = p4.shape
    nq = s // DIFF_T
    pairs = _diff_pairs(nq)
    qi_tab = jnp.asarray([p[0] for p in pairs], jnp.int32)
    kj_tab = jnp.asarray([p[1] for p in pairs], jnp.int32)

    def col(first):
        return pl.BlockSpec((None, None, s, LANES), lambda bi, h: (first + h, bi, 0, 0))

    smem = pl.BlockSpec(memory_space=pltpu.SMEM)
    kern = functools.partial(_diff_attn_kernel, out_scale=1.0 - lam_init)
    return pl.pallas_call(
        kern,
        grid=(b, A_HEADS),
        in_specs=[
            smem, smem, smem,
            col(COL_AQ), col(COL_AK), col(COL_AV), col(COL_AG),
            pl.BlockSpec((nq, None, DIFF_T, DIFF_T), lambda bi, h: (0, h, 0, 0)),
            pl.BlockSpec((LANES, 1), lambda bi, h: (0, 0)),
        ],
        out_specs=pl.BlockSpec((None, s, LANES), lambda bi, h: (bi, 0, h)),
        out_shape=jax.ShapeDtypeStruct((b, s, D_A), BF16),
        scratch_shapes=[
            pltpu.VMEM((nq, LANES + ONES_ROWS, DIFF_T), BF16),
            pltpu.VMEM((nq, LANES, 2 * DIFF_T), BF16),
            pltpu.VMEM((len(pairs), LANES + ONES_ROWS, 2 * DIFF_T), F32),
            pltpu.VMEM((len(pairs), 1, 2 * DIFF_T), F32),
        ],
        compiler_params=_params(("parallel", "parallel")),
        name="diff_attn",
    )(lam, qi_tab, kj_tab, p4, p4, p4, p4, bias, gain)


DIL_MERGE_ROWS = 256


def _dil_attn_kernel(q_ref, k_ref, v_ref, g_ref, bias_ref, o_ref,
                     qf, kf, vf, acc_s, m_s, l_s, *, seq, scale):
    qf[...] = q_ref[...].astype(F32)
    kf[...] = k_ref[...].astype(F32)
    vf[...] = v_ref[...].astype(F32)

    for p, dil in enumerate(DILATIONS):
        n_blocks = seq // dil // BLOCK
        bias = bias_ref[p]
        for r in range(dil):
            for j in range(n_blocks):
                def rows(jj, dil=dil, r=r):
                    start = dil * BLOCK * jj + r
                    if dil == 1:
                        return pl.ds(start, BLOCK)
                    return pl.ds(start, BLOCK, stride=dil)

                def take(ref16, ref32, jj, dil=dil):
                    if dil == 1:
                        return ref16[rows(jj), :]
                    return ref32[rows(jj), :].astype(BF16)

                qb = take(q_ref, qf, j)
                if j > 0:
                    kb = jnp.concatenate([take(k_ref, kf, j - 1), take(k_ref, kf, j)], axis=0)
                    vb = jnp.concatenate([take(v_ref, vf, j - 1), take(v_ref, vf, j)], axis=0)
                    b = bias
                else:
                    kb = take(k_ref, kf, j)
                    vb = take(v_ref, vf, j)
                    b = bias[:, BLOCK:]
                s = lax.dot_general(qb, kb, (((1,), (1,)), ((), ())),
                                    preferred_element_type=F32) * scale + b
                m = jnp.max(s, axis=-1, keepdims=True)
                e = jnp.exp(s - m)
                l = jnp.sum(e, axis=-1, keepdims=True)
                acc = jnp.dot(e.astype(BF16), vb, preferred_element_type=F32)
                acc_s[p, rows(j), :] = acc
                m_s[p, rows(j), :] = jnp.broadcast_to(m, (BLOCK, LANES))
                l_s[p, rows(j), :] = jnp.broadcast_to(l, (BLOCK, LANES))

    def merge(c, carry):
        r0 = pl.multiple_of(c * DIL_MERGE_ROWS, DIL_MERGE_ROWS)
        sl = pl.ds(r0, DIL_MERGE_ROWS)
        m0, m1, m2 = m_s[0, sl, :], m_s[1, sl, :], m_s[2, sl, :]
        m_all = jnp.maximum(jnp.maximum(m0, m1), m2)
        w0, w1, w2 = jnp.exp(m0 - m_all), jnp.exp(m1 - m_all), jnp.exp(m2 - m_all)
        num = w0 * acc_s[0, sl, :] + w1 * acc_s[1, sl, :] + w2 * acc_s[2, sl, :]
        den = w0 * l_s[0, sl, :] + w1 * l_s[1, sl, :] + w2 * l_s[2, sl, :]
        out = num / den
        o_ref[sl, :] = (out * _silu(g_ref[sl, :].astype(F32))).astype(BF16)
        return carry

    lax.fori_loop(0, seq // DIL_MERGE_ROWS, merge, 0)


def _dil_attn(p4, bias):
    _, b, s, _ = p4.shape
    n_pat = len(DILATIONS)

    def col(first):
        return lambda bi, h: (first + h, bi, 0, 0)

    head = pl.BlockSpec((None, None, s, LANES), col(COL_BQ))
    kern = functools.partial(_dil_attn_kernel, seq=s, scale=LANES ** -0.5)
    return pl.pallas_call(
        kern,
        grid=(b, B_HEADS),
        in_specs=[
            head,
            pl.BlockSpec((None, None, s, LANES), col(COL_BK)),
            pl.BlockSpec((None, None, s, LANES), col(COL_BV)),
            pl.BlockSpec((None, None, s, LANES), col(COL_BG)),
            pl.BlockSpec((n_pat, None, BLOCK, 2 * BLOCK), lambda bi, h: (0, h, 0, 0)),
        ],
        out_specs=pl.BlockSpec((None, s, LANES), lambda bi, h: (bi, 0, h)),
        out_shape=jax.ShapeDtypeStruct((b, s, D_B), BF16),
        scratch_shapes=[pltpu.VMEM((s, LANES), F32)] * 3
        + [pltpu.VMEM((n_pat, s, LANES), F32)] * 3,
        compiler_params=_params(("parallel", "parallel")),
        name="dil_attn",
    )(p4, p4, p4, p4, bias)


CONV_ROWS = 256


def _conv_kernel(u_ref, glu_ref, g_ref, dw_ref, db_ref, lng_ref, lnb_ref, pw_ref, o_ref,
                 hpad, hc, *, seq):
    for c in range(C_BLOCKS):
        hpad[c, pl.ds(0, CONV_PAD), :] = jnp.zeros((CONV_PAD, LANES), F32)
        u = u_ref[c].astype(F32)
        gate = glu_ref[c].astype(F32)
        hpad[c, pl.ds(CONV_PAD, seq), :] = u * (1.0 / (1.0 + jnp.exp(-gate)))

    first = CONV_PAD - (CONV_WIDTH - 1)
    for c in range(C_BLOCKS):
        cols = slice(c * LANES, (c + 1) * LANES)

        def conv_chunk(ci, carry, c=c, cols=cols):
            r0 = pl.multiple_of(ci * CONV_ROWS, CONV_ROWS)
            win = hpad[c, pl.ds(r0, CONV_ROWS + CONV_PAD), :]
            acc = jnp.broadcast_to(db_ref[:, cols], (CONV_ROWS, LANES))
            for j in range(CONV_WIDTH):
                acc = acc + dw_ref[pl.ds(j, 1), cols] * win[first + j:first + j + CONV_ROWS, :]
            hc[pl.ds(r0, CONV_ROWS), cols] = acc
            return carry

        lax.fori_loop(0, seq // CONV_ROWS, conv_chunk, 0)

    def tail(ci, carry):
        r0 = pl.multiple_of(ci * CONV_ROWS, CONV_ROWS)
        sl = pl.ds(r0, CONV_ROWS)
        x = hc[sl, :]
        mu = jnp.mean(x, axis=-1, keepdims=True)
        xc = x - mu
        var = jnp.mean(xc * xc, axis=-1, keepdims=True)
        y = xc * lax.rsqrt(var + EPS) * lng_ref[...] + lnb_ref[...]
        y = _silu(y)
        z = jnp.dot(y.astype(BF16), pw_ref[...], preferred_element_type=F32)
        for c in range(C_BLOCKS):
            cols = slice(c * LANES, (c + 1) * LANES)
            o_ref[sl, cols] = (z[:, cols] * _silu(g_ref[c, sl, :].astype(F32))).astype(BF16)
        return carry

    lax.fori_loop(0, seq // CONV_ROWS, tail, 0)


def _conv_mixer(p4, conv_dw, conv_b, ln_g, ln_b, pw_bf16):
    _, b, s, _ = p4.shape

    def cols(first):
        return lambda bi: (first // C_BLOCKS, bi, 0, 0)

    def whole(shape):
        return pl.BlockSpec(shape, lambda bi: (0,) * len(shape))

    blocks = (C_BLOCKS, None, s, LANES)
    return pl.pallas_call(
        functools.partial(_conv_kernel, seq=s),
        grid=(b,),
        in_specs=[
            pl.BlockSpec(blocks, cols(COL_CU)),
            pl.BlockSpec(blocks, cols(COL_CGLU)),
            pl.BlockSpec(blocks, cols(COL_CG)),
            whole((CONV_WIDTH, D_C)),
            whole((1, D_C)),
            whole((1, D_C)),
            whole((1, D_C)),
            whole((D_C, D_C)),
        ],
        out_specs=pl.BlockSpec((None, s, D_C), lambda bi: (bi, 0, 0)),
        out_shape=jax.ShapeDtypeStruct((b, s, D_C), BF16),
        scratch_shapes=[pltpu.VMEM((C_BLOCKS, CONV_PAD + s, LANES), F32),
                        pltpu.VMEM((s, D_C), F32)],
        compiler_params=_params(("parallel",)),
        name="conv_mixer",
    )(p4, p4, p4, conv_dw, conv_b.reshape(1, D_C), ln_g.reshape(1, D_C), ln_b.reshape(1, D_C),
      pw_bf16)


OUT_TM = 512


def _out_proj_kernel(ya_ref, yb_ref, yc_ref, x_ref, w_ref, g_ref, b_ref, o_ref, *, alpha):
    y = jnp.dot(ya_ref[...], w_ref[pl.ds(0, D_A), :], preferred_element_type=F32)
    y = y + jnp.dot(yb_ref[...], w_ref[pl.ds(D_A, D_B), :], preferred_element_type=F32)
    y = y + jnp.dot(yc_ref[...], w_ref[pl.ds(D_A + D_B, D_C), :], preferred_element_type=F32)
    z = alpha * x_ref[...] + y
    mu = jnp.mean(z, axis=-1, keepdims=True)
    zc = z - mu
    var = jnp.mean(zc * zc, axis=-1, keepdims=True)
    o_ref[...] = zc * lax.rsqrt(var + EPS) * g_ref[...] + b_ref[...]


def _out_proj(ya, yb, yc, x2d, w_bf16, ln_g, ln_b, alpha):
    m = x2d.shape[0]

    def rows(width):
        return pl.BlockSpec((OUT_TM, width), lambda i: (i, 0))

    def whole(shape):
        return pl.BlockSpec(shape, lambda i: (0, 0))

    return pl.pallas_call(
        functools.partial(_out_proj_kernel, alpha=alpha),
        grid=(m // OUT_TM,),
        in_specs=[rows(D_A), rows(D_B), rows(D_C), rows(D_MODEL),
                  whole((D_MODEL, D_MODEL)), whole((1, D_MODEL)), whole((1, D_MODEL))],
        out_specs=rows(D_MODEL),
        out_shape=jax.ShapeDtypeStruct((m, D_MODEL), F32),
        compiler_params=_params(("parallel",)),
        name="out_proj",
    )(ya, yb, yc, x2d, w_bf16, ln_g.reshape(1, D_MODEL), ln_b.reshape(1, D_MODEL))


def _in_col_scale():
    scale = jnp.ones((D_IN,), F32)
    return scale.at[COL_AQ * LANES:COL_AQ * LANES + D_A].set((LANES // 2) ** -0.5 * LOG2E)


def _layer(x2d, batch, seq, layer, depth, w_in, diff_lambda, diff_head_gain, conv_dw, conv_b,
           conv_ln_g, conv_ln_b, conv_pw, w_out, ln_g, ln_b, diff_bias, dil_bias):
    m = batch * seq
    lam_init = 0.8 - 0.6 * math.exp(-0.3 * layer)
    alpha = (2 * depth) ** 0.25
    proj = _in_proj(x2d, (w_in * _in_col_scale()).astype(BF16))
    p4 = proj.reshape(N_COL_BLOCKS, batch, seq, LANES)

    lam_v = diff_lambda.astype(F32)
    lam = (jnp.exp(jnp.sum(lam_v[0] * lam_v[1])) - jnp.exp(jnp.sum(lam_v[2] * lam_v[3]))
           + lam_init).reshape(1)
    ya = _diff_attn(p4, diff_bias, lam, diff_head_gain.astype(F32).reshape(LANES, 1), lam_init)
    yb = _dil_attn(p4, dil_bias)
    yc = _conv_mixer(p4, conv_dw, conv_b, conv_ln_g, conv_ln_b, conv_pw.astype(BF16))
    return _out_proj(ya.reshape(m, D_A), yb.reshape(m, D_B), yc.reshape(m, D_C), x2d,
                     w_out.astype(BF16), ln_g, ln_b, alpha)


def kernel(x, w_in, diff_lambda, diff_head_gain, conv_dw, conv_b, conv_ln_g, conv_ln_b, conv_pw,
           w_out, ln_g, ln_b, rel_bias):
    batch, seq, _ = x.shape
    depth = w_in.shape[0]
    diff_bias = _diff_bias_tiles(rel_bias, seq)
    dil_bias = _dil_bias_tiles(rel_bias)
    x2d = x.reshape(batch * seq, D_MODEL)
    for layer in range(depth):
        x2d = _layer(x2d, batch, seq, layer, depth, w_in[layer], diff_lambda[layer],
                     diff_head_gain[layer], conv_dw[layer], conv_b[layer], conv_ln_g[layer],
                     conv_ln_b[layer], conv_pw[layer], w_out[layer], ln_g[layer], ln_b[layer],
                     diff_bias, dil_bias)
    return x2d.reshape(batch, seq, D_MODEL)
```

```python
import functools
import math

import jax
import jax.numpy as jnp
from jax import lax
from jax.experimental import pallas as pl
from jax.experimental.pallas import tpu as pltpu

F32 = jnp.float32
BF16 = jnp.bfloat16

LANES = 128
D_MODEL = 2048
A_HEADS = 4
B_HEADS = 8
D_A = A_HEADS * LANES
D_B = B_HEADS * LANES
D_C = D_MODEL - D_A - D_B
C_BLOCKS = D_C // LANES
CONV_WIDTH = 31
CONV_PAD = 32
DILATIONS = (1, 4, 16)
BLOCK = 128
N_BUCKETS = 32
MAX_DISTANCE = 2048
EPS = 1e-5
LOG2E = math.log2(math.e)
D_IN = 4 * D_A + 4 * D_B + 3 * D_C
N_COL_BLOCKS = D_IN // LANES
COL_AQ, COL_AK, COL_AV, COL_AG = 0, 4, 8, 12
COL_BQ, COL_BK, COL_BV, COL_BG = 16, 24, 32, 40
COL_CU, COL_CGLU, COL_CG = 48, 52, 56

VMEM_LIMIT_BYTES = 56 * 1024 * 1024


def _params(semantics):
    return pltpu.CompilerParams(dimension_semantics=semantics, vmem_limit_bytes=VMEM_LIMIT_BYTES)


def _silu(x):
    return x * (1.0 / (1.0 + jnp.exp(-x)))


IN_TM = 1024
IN_TN = 1280


def _in_proj_kernel(x_ref, w_ref, o_ref, xb_ref):
    @pl.when(pl.program_id(1) == 0)
    def _():
        xb_ref[...] = x_ref[...].astype(BF16)

    acc = jnp.dot(xb_ref[...], w_ref[...], preferred_element_type=F32)
    for c in range(IN_TN // LANES):
        o_ref[c] = acc[:, c * LANES:(c + 1) * LANES].astype(BF16)


def _in_proj(x2d, w_bf16):
    m = x2d.shape[0]
    nb = IN_TN // LANES
    return pl.pallas_call(
        _in_proj_kernel,
        grid=(m // IN_TM, D_IN // IN_TN),
        in_specs=[
            pl.BlockSpec((IN_TM, D_MODEL), lambda i, j: (i, 0)),
            pl.BlockSpec((D_MODEL, IN_TN), lambda i, j: (0, j)),
        ],
        out_specs=pl.BlockSpec((nb, IN_TM, LANES), lambda i, j: (j, i, 0)),
        out_shape=jax.ShapeDtypeStruct((N_COL_BLOCKS, m, LANES), BF16),
        scratch_shapes=[pltpu.VMEM((IN_TM, D_MODEL), BF16)],
        compiler_params=_params(("parallel", "arbitrary")),
        name="in_proj",
    )(x2d, w_bf16)


def _t5_bucket(dist):
    max_exact = N_BUCKETS // 2
    d = jnp.maximum(dist, 0)
    df = jnp.maximum(d, 1).astype(F32)
    large = max_exact + (jnp.log(df / max_exact) / math.log(MAX_DISTANCE / max_exact)
                         * (N_BUCKETS - max_exact)).astype(jnp.int32)
    large = jnp.minimum(large, N_BUCKETS - 1)
    return jnp.where(d < max_exact, d, large)


DIFF_T = 256


def _bucket_lookup(table, bucket):
    onehot = bucket[..., None] == jnp.arange(N_BUCKETS)
    cols = [jnp.sum(jnp.where(onehot, table[:, h].astype(F32), 0.0), axis=-1)
            for h in range(table.shape[1])]
    return jnp.stack(cols)


def _diff_bias_tiles(rel_bias, seq):
    nd = seq // DIFF_T
    key = jnp.arange(DIFF_T)[:, None]
    qry = jnp.arange(DIFF_T)[None, :]
    dist = jnp.arange(nd)[:, None, None] * DIFF_T + qry - key
    tiles = _bucket_lookup(rel_bias[:, :A_HEADS], _t5_bucket(jnp.clip(dist, 0, seq - 1)))
    tiles = jnp.where(dist[None] >= 0, tiles * LOG2E, -jnp.inf)
    return jnp.transpose(tiles, (1, 0, 2, 3))


def _dil_bias_tiles(rel_bias):
    qi = jnp.arange(BLOCK)[:, None]
    kj = jnp.arange(2 * BLOCK)[None, :]
    lag = BLOCK + qi - kj
    valid = (lag >= 0) & (lag <= BLOCK)
    tiles = []
    for dil in DILATIONS:
        b = _bucket_lookup(rel_bias[:, A_HEADS:],
                           _t5_bucket(jnp.clip(lag, 0, 2 * BLOCK - 1) * dil))
        tiles.append(jnp.where(valid[None], b * LOG2E, -jnp.inf))
        tiles.append(jnp.where((valid & (kj >= BLOCK))[None], b * LOG2E, -jnp.inf))
    return jnp.stack(tiles)


DIFF_UNROLL = 2
ONES_ROWS = 16


def _diff_pairs(nq):
    return [(qi, kj) for qi in range(nq) for kj in range(qi + 1)]


def _diff_attn_kernel(lam_ref, qi_ref, kj_ref, q_ref, k_ref, v_ref, g_ref, bias_ref, gain_ref,
                      o_ref, vt_s, q2t_s, acc_s, m_s, *, out_scale):
    t = DIFF_T
    nq = q_ref.shape[0] // t
    n_pairs = len(_diff_pairs(nq))
    u = DIFF_UNROLL
    assert n_pairs % u == 0 and n_pairs >= 2 * u

    half = lax.broadcasted_iota(jnp.int32, (LANES, t), 0) < LANES // 2
    ones = jnp.ones((ONES_ROWS, t), BF16)
    for j in range(nq):
        rows = pl.ds(j * t, t)
        vt_s[j] = jnp.concatenate([v_ref[rows, :].astype(F32).T.astype(BF16), ones], axis=0)
        qt = q_ref[rows, :].astype(F32).T
        q2t_s[j] = jnp.concatenate([jnp.where(half, qt, 0.0), jnp.where(half, 0.0, qt)],
                                   axis=1).astype(BF16)

    def probs(n):
        qi, kj = qi_ref[n], kj_ref[n]
        k = k_ref[pl.ds(pl.multiple_of(kj * t, t), t), :]
        b = bias_ref[qi - kj]
        s = jnp.dot(k, q2t_s[qi], preferred_element_type=F32)
        s = s + jnp.concatenate([b, b], axis=1)
        m_blk = jnp.max(s, axis=0, keepdims=True)
        m_s[n] = m_blk
        return jnp.exp2(s - m_blk).astype(BF16)

    def values(n, p):
        acc_s[n] = jnp.dot(vt_s[kj_ref[n]], p, preferred_element_type=F32)

    def body(it, carry):
        n = it * u
        nxt = tuple(probs(n + u + c) for c in range(u))
        for c in range(u):
            values(n + c, carry[c])
        return nxt

    carry = lax.fori_loop(0, n_pairs // u - 1, body, tuple(probs(c) for c in range(u)))
    for c in range(u):
        values(n_pairs - u + c, carry[c])

    lam = lam_ref[0]
    for qi in range(nq):
        first = qi * (qi + 1) // 2
        ms = [m_s[first + j] for j in range(qi + 1)]
        m_all = functools.reduce(jnp.maximum, ms)
        acc = jnp.zeros((LANES + ONES_ROWS, 2 * t), F32)
        for j in range(qi + 1):
            acc = acc + jnp.exp2(ms[j] - m_all) * acc_s[first + j]
        o = acc[:LANES] / acc[LANES:LANES + 1]
        out = o[:, :t] - lam * o[:, t:]
        out = out * lax.rsqrt(jnp.mean(out * out, axis=0, keepdims=True) + EPS)
        out = (out * gain_ref[...] * out_scale).T
        rows = pl.ds(qi * t, t)
        o_ref[rows, :] = (out * _silu(g_ref[rows, :].astype(F32))).astype(BF16)


def _diff_attn(p4, bias, lam, gain, lam_init):
    _, b, s, _ = p4.shape
    nq = s // DIFF_T
    pairs = _diff_pairs(nq)
    qi_tab = jnp.asarray([p[0] for p in pairs], jnp.int32)
    kj_tab = jnp.asarray([p[1] for p in pairs], jnp.int32)

    def col(first):
        return pl.BlockSpec((None, None, s, LANES), lambda bi, h: (first + h, bi, 0, 0))

    smem = pl.BlockSpec(memory_space=pltpu.SMEM)
    kern = functools.partial(_diff_attn_kernel, out_scale=1.0 - lam_init)
    return pl.pallas_call(
        kern,
        grid=(b, A_HEADS),
        in_specs=[
            smem, smem, smem,
            col(COL_AQ), col(COL_AK), col(COL_AV), col(COL_AG),
            pl.BlockSpec((nq, None, DIFF_T, DIFF_T), lambda bi, h: (0, h, 0, 0)),
            pl.BlockSpec((LANES, 1), lambda bi, h: (0, 0)),
        ],
        out_specs=pl.BlockSpec((None, s, LANES), lambda bi, h: (bi, 0, h)),
        out_shape=jax.ShapeDtypeStruct((b, s, D_A), BF16),
        scratch_shapes=[
            pltpu.VMEM((nq, LANES + ONES_ROWS, DIFF_T), BF16),
            pltpu.VMEM((nq, LANES, 2 * DIFF_T), BF16),
            pltpu.VMEM((len(pairs), LANES + ONES_ROWS, 2 * DIFF_T), F32),
            pltpu.VMEM((len(pairs), 1, 2 * DIFF_T), F32),
        ],
        compiler_params=_params(("parallel", "parallel")),
        name="diff_attn",
    )(lam, qi_tab, kj_tab, p4, p4, p4, p4, bias, gain)


DIL_UNROLL = 8
STEP = 4


def _dil_variants(seq):
    out = []
    for p, dil in enumerate(DILATIONS):
        per_residue = seq // dil // BLOCK
        for _ in range(dil):
            out += [2 * p + (1 if j == 0 else 0) for j in range(per_residue)]
    return out


def _dil_split(seq, p):
    out = []
    prev_len, length = seq // DILATIONS[p - 1], seq // DILATIONS[p]
    for r_prev in range(DILATIONS[p - 1]):
        for b in range(STEP):
            r = r_prev + DILATIONS[p - 1] * b
            out.append((r * length, pl.ds(r_prev * prev_len + b, length, stride=STEP)))
    return out


def _dil_attn_kernel(var_ref, q_ref, k_ref, v_ref, g_ref, bias_ref, o_ref,
                     f32_a, f32_b, qa, ka, va, acc_s, den_s, mst, out_s, *, seq):
    n_pat = len(DILATIONS)
    n_blocks = n_pat * seq // BLOCK
    u = DIL_UNROLL
    assert n_blocks % u == 0 and n_blocks >= 2 * u

    ka[pl.ds(0, BLOCK), :] = jnp.zeros((BLOCK, LANES), BF16)
    va[pl.ds(0, BLOCK), :] = jnp.zeros((BLOCK, 2 * LANES), BF16)
    va[pl.ds(BLOCK, n_pat * seq), pl.ds(LANES, LANES)] = jnp.ones((n_pat * seq, LANES), BF16)
    for src, dst, front in ((q_ref, qa, 0), (k_ref, ka, BLOCK), (v_ref, va, BLOCK)):
        dst[pl.ds(front, seq), pl.ds(0, LANES)] = src[...]
        cur, nxt = f32_a, f32_b
        cur[...] = src[...].astype(F32)
        for p in range(1, n_pat):
            for row, piece in _dil_split(seq, p):
                x = cur[piece, :]
                if p + 1 < n_pat:
                    nxt[pl.ds(row, seq // DILATIONS[p]), :] = x
                dst[pl.ds(front + p * seq + row, seq // DILATIONS[p]), pl.ds(0, LANES)] = (
                    x.astype(BF16))
            cur, nxt = nxt, cur

    def probs(g):
        base = pl.multiple_of(g * BLOCK, BLOCK)
        s = lax.dot_general(qa[pl.ds(base, BLOCK), :], ka[pl.ds(base, 2 * BLOCK), :],
                            (((1,), (1,)), ((), ())), preferred_element_type=F32)
        s = s + bias_ref[var_ref[g]]
        m = jnp.max(s, axis=1, keepdims=True)
        mst[pl.ds(base, BLOCK), :] = jnp.broadcast_to(m, (BLOCK, LANES))
        return jnp.exp2(s - m).astype(BF16)

    def values(g, p):
        base = pl.multiple_of(g * BLOCK, BLOCK)
        pv = jnp.dot(p, va[pl.ds(base, 2 * BLOCK), :], preferred_element_type=F32)
        acc_s[pl.ds(base, BLOCK), :] = pv[:, :LANES]
        den_s[pl.ds(base, BLOCK), :] = pv[:, LANES:]

    def body(it, carry):
        g = it * u
        nxt = tuple(probs(g + u + c) for c in range(u))
        for c in range(u):
            values(g + c, carry[c])
        return nxt

    carry = lax.fori_loop(0, n_blocks // u - 1, body, tuple(probs(c) for c in range(u)))
    for c in range(u):
        values(n_blocks - u + c, carry[c])

    for p in range(n_pat - 1, 0, -1):
        length = seq // DILATIONS[p]
        for row, piece in _dil_split(seq, p):
            for k in range(length // BLOCK):
                fine = pl.ds(p * seq + row + k * BLOCK, BLOCK)
                coarse = pl.ds((p - 1) * seq + piece.start + STEP * BLOCK * k, BLOCK, stride=STEP)
                m_a, m_b = mst[coarse, :], mst[fine, :]
                m_ab = jnp.maximum(m_a, m_b)
                w_a, w_b = jnp.exp2(m_a - m_ab), jnp.exp2(m_b - m_ab)
                acc = w_a * acc_s[coarse, :] + w_b * acc_s[fine, :]
                den = w_a * den_s[coarse, :] + w_b * den_s[fine, :]
                if p > 1:
                    mst[coarse, :] = m_ab
                    acc_s[coarse, :] = acc
                    den_s[coarse, :] = den
                else:
                    out_s[coarse, :] = acc / den
    o_ref[...] = (out_s[...] * _silu(g_ref[...].astype(F32))).astype(BF16)


def _dil_attn(p4, bias):
    _, b, s, _ = p4.shape
    assert all(b == a * STEP for a, b in zip(DILATIONS, DILATIONS[1:])) and DILATIONS[0] == 1
    n_pat = len(DILATIONS)
    variants = jnp.asarray(_dil_variants(s), jnp.int32)

    def col(first):
        return pl.BlockSpec((None, None, s, LANES), lambda bi, h: (first + h, bi, 0, 0))

    rows = n_pat * s
    return pl.pallas_call(
        functools.partial(_dil_attn_kernel, seq=s),
        grid=(b, B_HEADS),
        in_specs=[
            pl.BlockSpec(memory_space=pltpu.SMEM),
            col(COL_BQ), col(COL_BK), col(COL_BV), col(COL_BG),
            pl.BlockSpec((2 * n_pat, None, BLOCK, 2 * BLOCK), lambda bi, h: (0, h, 0, 0)),
        ],
        out_specs=pl.BlockSpec((None, s, LANES), lambda bi, h: (bi, 0, h)),
        out_shape=jax.ShapeDtypeStruct((b, s, D_B), BF16),
        scratch_shapes=[
            pltpu.VMEM((s, LANES), F32),
            pltpu.VMEM((s, LANES), F32),
            pltpu.VMEM((rows, LANES), BF16),
            pltpu.VMEM((BLOCK + rows, LANES), BF16),
            pltpu.VMEM((BLOCK + rows, 2 * LANES), BF16),
            pltpu.VMEM((rows, LANES), F32),
            pltpu.VMEM((rows, LANES), F32),
            pltpu.VMEM((rows, LANES), F32),
            pltpu.VMEM((s, LANES), F32),
        ],
        compiler_params=_params(("parallel", "parallel")),
        name="dil_attn",
    )(variants, p4, p4, p4, p4, bias)


CONV_ROWS = 256


def _conv_kernel(u_ref, glu_ref, g_ref, dw_ref, db_ref, lng_ref, lnb_ref, pw_ref, o_ref,
                 hpad, hc, *, seq):
    for c in range(C_BLOCKS):
        hpad[c, pl.ds(0, CONV_PAD), :] = jnp.zeros((CONV_PAD, LANES), F32)
        u = u_ref[c].astype(F32)
        gate = glu_ref[c].astype(F32)
        hpad[c, pl.ds(CONV_PAD, seq), :] = u * (1.0 / (1.0 + jnp.exp(-gate)))

    first = CONV_PAD - (CONV_WIDTH - 1)
    for c in range(C_BLOCKS):
        cols = slice(c * LANES, (c + 1) * LANES)

        def conv_chunk(ci, carry, c=c, cols=cols):
            r0 = pl.multiple_of(ci * CONV_ROWS, CONV_ROWS)
            win = hpad[c, pl.ds(r0, CONV_ROWS + CONV_PAD), :]
            acc = jnp.broadcast_to(db_ref[:, cols], (CONV_ROWS, LANES))
            for j in range(CONV_WIDTH):
                acc = acc + dw_ref[pl.ds(j, 1), cols] * win[first + j:first + j + CONV_ROWS, :]
            hc[pl.ds(r0, CONV_ROWS), cols] = acc
            return carry

        lax.fori_loop(0, seq // CONV_ROWS, conv_chunk, 0)

    def tail(ci, carry):
        r0 = pl.multiple_of(ci * CONV_ROWS, CONV_ROWS)
        sl = pl.ds(r0, CONV_ROWS)
        x = hc[sl, :]
        mu = jnp.mean(x, axis=-1, keepdims=True)
        xc = x - mu
        var = jnp.mean(xc * xc, axis=-1, keepdims=True)
        y = xc * lax.rsqrt(var + EPS) * lng_ref[...] + lnb_ref[...]
        y = _silu(y)
        z = jnp.dot(y.astype(BF16), pw_ref[...], preferred_element_type=F32)
        for c in range(C_BLOCKS):
            cols = slice(c * LANES, (c + 1) * LANES)
            o_ref[sl, cols] = (z[:, cols] * _silu(g_ref[c, sl, :].astype(F32))).astype(BF16)
        return carry

    lax.fori_loop(0, seq // CONV_ROWS, tail, 0)


def _conv_mixer(p4, conv_dw, conv_b, ln_g, ln_b, pw_bf16):
    _, b, s, _ = p4.shape

    def cols(first):
        return lambda bi: (first // C_BLOCKS, bi, 0, 0)

    def whole(shape):
        return pl.BlockSpec(shape, lambda bi: (0,) * len(shape))

    blocks = (C_BLOCKS, None, s, LANES)
    return pl.pallas_call(
        functools.partial(_conv_kernel, seq=s),
        grid=(b,),
        in_specs=[
            pl.BlockSpec(blocks, cols(COL_CU)),
            pl.BlockSpec(blocks, cols(COL_CGLU)),
            pl.BlockSpec(blocks, cols(COL_CG)),
            whole((CONV_WIDTH, D_C)),
            whole((1, D_C)),
            whole((1, D_C)),
            whole((1, D_C)),
            whole((D_C, D_C)),
        ],
        out_specs=pl.BlockSpec((None, s, D_C), lambda bi: (bi, 0, 0)),
        out_shape=jax.ShapeDtypeStruct((b, s, D_C), BF16),
        scratch_shapes=[pltpu.VMEM((C_BLOCKS, CONV_PAD + s, LANES), F32),
                        pltpu.VMEM((s, D_C), F32)],
        compiler_params=_params(("parallel",)),
        name="conv_mixer",
    )(p4, p4, p4, conv_dw, conv_b.reshape(1, D_C), ln_g.reshape(1, D_C), ln_b.reshape(1, D_C),
      pw_bf16)


OUT_TM = 512


def _out_proj_kernel(ya_ref, yb_ref, yc_ref, x_ref, w_ref, g_ref, b_ref, o_ref, *, alpha):
    y = jnp.dot(ya_ref[...], w_ref[pl.ds(0, D_A), :], preferred_element_type=F32)
    y = y + jnp.dot(yb_ref[...], w_ref[pl.ds(D_A, D_B), :], preferred_element_type=F32)
    y = y + jnp.dot(yc_ref[...], w_ref[pl.ds(D_A + D_B, D_C), :], preferred_element_type=F32)
    z = alpha * x_ref[...] + y
    mu = jnp.mean(z, axis=-1, keepdims=True)
    zc = z - mu
    var = jnp.mean(zc * zc, axis=-1, keepdims=True)
    o_ref[...] = zc * lax.rsqrt(var + EPS) * g_ref[...] + b_ref[...]


def _out_proj(ya, yb, yc, x2d, w_bf16, ln_g, ln_b, alpha):
    m = x2d.shape[0]

    def rows(width):
        return pl.BlockSpec((OUT_TM, width), lambda i: (i, 0))

    def whole(shape):
        return pl.BlockSpec(shape, lambda i: (0, 0))

    return pl.pallas_call(
        functools.partial(_out_proj_kernel, alpha=alpha),
        grid=(m // OUT_TM,),
        in_specs=[rows(D_A), rows(D_B), rows(D_C), rows(D_MODEL),
                  whole((D_MODEL, D_MODEL)), whole((1, D_MODEL)), whole((1, D_MODEL))],
        out_specs=rows(D_MODEL),
        out_shape=jax.ShapeDtypeStruct((m, D_MODEL), F32),
        compiler_params=_params(("parallel",)),
        name="out_proj",
    )(ya, yb, yc, x2d, w_bf16, ln_g.reshape(1, D_MODEL), ln_b.reshape(1, D_MODEL))


def _in_col_scale():
    scale = jnp.ones((D_IN,), F32)
    scale = scale.at[COL_AQ * LANES:COL_AQ * LANES + D_A].set((LANES // 2) ** -0.5 * LOG2E)
    return scale.at[COL_BQ * LANES:COL_BQ * LANES + D_B].set(LANES ** -0.5 * LOG2E)


def _layer(x2d, batch, seq, layer, depth, w_in, diff_lambda, diff_head_gain, conv_dw, conv_b,
           conv_ln_g, conv_ln_b, conv_pw, w_out, ln_g, ln_b, diff_bias, dil_bias):
    m = batch * seq
    lam_init = 0.8 - 0.6 * math.exp(-0.3 * layer)
    alpha = (2 * depth) ** 0.25
    proj = _in_proj(x2d, (w_in * _in_col_scale()).astype(BF16))
    p4 = proj.reshape(N_COL_BLOCKS, batch, seq, LANES)

    lam_v = diff_lambda.astype(F32)
    lam = (jnp.exp(jnp.sum(lam_v[0] * lam_v[1])) - jnp.exp(jnp.sum(lam_v[2] * lam_v[3]))
           + lam_init).reshape(1)
    ya = _diff_attn(p4, diff_bias, lam, diff_head_gain.astype(F32).reshape(LANES, 1), lam_init)
    yb = _dil_attn(p4, dil_bias)
    yc = _conv_mixer(p4, conv_dw, conv_b, conv_ln_g, conv_ln_b, conv_pw.astype(BF16))
    return _out_proj(ya.reshape(m, D_A), yb.reshape(m, D_B), yc.reshape(m, D_C), x2d,
                     w_out.astype(BF16), ln_g, ln_b, alpha)


def kernel(x, w_in, diff_lambda, diff_head_gain, conv_dw, conv_b, conv_ln_g, conv_ln_b, conv_pw,
           w_out, ln_g, ln_b, rel_bias):
    batch, seq, _ = x.shape
    depth = w_in.shape[0]
    diff_bias = _diff_bias_tiles(rel_bias, seq)
    dil_bias = _dil_bias_tiles(rel_bias)
    x2d = x.reshape(batch * seq, D_MODEL)
    for layer in range(depth):
        x2d = _layer(x2d, batch, seq, layer, depth, w_in[layer], diff_lambda[layer],
                     diff_head_gain[layer], conv_dw[layer], conv_b[layer], conv_ln_g[layer],
                     conv_ln_b[layer], conv_pw[layer], w_out[layer], ln_g[layer], ln_b[layer],
                     diff_bias, dil_bias)
    return x2d.reshape(batch, seq, D_MODEL)
```

```python
import functools
import math

import jax
import jax.numpy as jnp
from jax import lax
from jax.experimental import pallas as pl
from jax.experimental.pallas import tpu as pltpu

F32 = jnp.float32
BF16 = jnp.bfloat16

LANES = 128
D_MODEL = 2048
A_HEADS = 4
B_HEADS = 8
D_A = A_HEADS * LANES
D_B = B_HEADS * LANES
D_C = D_MODEL - D_A - D_B
C_BLOCKS = D_C // LANES
CONV_WIDTH = 31
CONV_PAD = 32
DILATIONS = (1, 4, 16)
BLOCK = 128
N_BUCKETS = 32
MAX_DISTANCE = 2048
EPS = 1e-5
LOG2E = math.log2(math.e)
D_IN = 4 * D_A + 4 * D_B + 3 * D_C
N_COL_BLOCKS = D_IN // LANES
COL_AQ, COL_AK, COL_AV, COL_AG = 0, 4, 8, 12
COL_BQ, COL_BK, COL_BV, COL_BG = 16, 24, 32, 40
COL_CU, COL_CGLU, COL_CG = 48, 52, 56

VMEM_LIMIT_BYTES = 56 * 1024 * 1024


def _params(semantics):
    return pltpu.CompilerParams(dimension_semantics=semantics, vmem_limit_bytes=VMEM_LIMIT_BYTES)


def _silu(x):
    return x * (1.0 / (1.0 + jnp.exp(-x)))


IN_TM = 1024
IN_TN = 1280


def _in_proj_kernel(x_ref, w_ref, o_ref, xb_ref):
    @pl.when(pl.program_id(1) == 0)
    def _():
        xb_ref[...] = x_ref[...].astype(BF16)

    acc = jnp.dot(xb_ref[...], w_ref[...], preferred_element_type=F32)
    for c in range(IN_TN // LANES):
        o_ref[c] = acc[:, c * LANES:(c + 1) * LANES].astype(BF16)


def _in_proj(x2d, w_bf16):
    m = x2d.shape[0]
    nb = IN_TN // LANES
    return pl.pallas_call(
        _in_proj_kernel,
        grid=(m // IN_TM, D_IN // IN_TN),
        in_specs=[
            pl.BlockSpec((IN_TM, D_MODEL), lambda i, j: (i, 0)),
            pl.BlockSpec((D_MODEL, IN_TN), lambda i, j: (0, j)),
        ],
        out_specs=pl.BlockSpec((nb, IN_TM, LANES), lambda i, j: (j, i, 0)),
        out_shape=jax.ShapeDtypeStruct((N_COL_BLOCKS, m, LANES), BF16),
        scratch_shapes=[pltpu.VMEM((IN_TM, D_MODEL), BF16)],
        compiler_params=_params(("parallel", "arbitrary")),
        name="in_proj",
    )(x2d, w_bf16)


def _t5_bucket(dist):
    max_exact = N_BUCKETS // 2
    d = jnp.maximum(dist, 0)
    df = jnp.maximum(d, 1).astype(F32)
    large = max_exact + (jnp.log(df / max_exact) / math.log(MAX_DISTANCE / max_exact)
                         * (N_BUCKETS - max_exact)).astype(jnp.int32)
    large = jnp.minimum(large, N_BUCKETS - 1)
    return jnp.where(d < max_exact, d, large)


DIFF_T = 256


def _bucket_lookup(table, bucket):
    onehot = bucket[..., None] == jnp.arange(N_BUCKETS)
    cols = [jnp.sum(jnp.where(onehot, table[:, h].astype(F32), 0.0), axis=-1)
            for h in range(table.shape[1])]
    return jnp.stack(cols)


def _diff_bias_tiles(rel_bias, seq):
    nd = seq // DIFF_T
    key = jnp.arange(DIFF_T)[:, None]
    qry = jnp.arange(DIFF_T)[None, :]
    dist = jnp.arange(nd)[:, None, None] * DIFF_T + qry - key
    tiles = _bucket_lookup(rel_bias[:, :A_HEADS], _t5_bucket(jnp.clip(dist, 0, seq - 1)))
    tiles = jnp.where(dist[None] >= 0, tiles * LOG2E, -jnp.inf)
    return jnp.transpose(tiles, (1, 0, 2, 3))


def _dil_bias_tiles(rel_bias):
    qi = jnp.arange(BLOCK)[:, None]
    kj = jnp.arange(2 * BLOCK)[None, :]
    lag = BLOCK + qi - kj
    valid = (lag >= 0) & (lag <= BLOCK)
    tiles = []
    for dil in DILATIONS:
        b = _bucket_lookup(rel_bias[:, A_HEADS:],
                           _t5_bucket(jnp.clip(lag, 0, 2 * BLOCK - 1) * dil))
        tiles.append(jnp.where(valid[None], b * LOG2E, -jnp.inf))
        tiles.append(jnp.where((valid & (kj >= BLOCK))[None], b * LOG2E, -jnp.inf))
    return jnp.stack(tiles)


DIFF_UNROLL = 2
ONES_ROWS = 16


def _diff_pairs(nq):
    return [(qi, kj) for qi in range(nq) for kj in range(qi + 1)]


def _diff_attn_kernel(lam_ref, qi_ref, kj_ref, q_ref, k_ref, v_ref, g_ref, bias_ref, gain_ref,
                      o_ref, vt_s, q2t_s, acc_s, m_s, *, out_scale):
    t = DIFF_T
    nq = q_ref.shape[0] // t
    n_pairs = len(_diff_pairs(nq))
    u = DIFF_UNROLL
    assert n_pairs % u == 0 and n_pairs >= 2 * u

    half = lax.broadcasted_iota(jnp.int32, (LANES, t), 0) < LANES // 2
    ones = jnp.ones((ONES_ROWS, t), BF16)
    for j in range(nq):
        rows = pl.ds(j * t, t)
        vt_s[j] = jnp.concatenate([v_ref[rows, :].astype(F32).T.astype(BF16), ones], axis=0)
        qt = q_ref[rows, :].astype(F32).T
        q2t_s[j] = jnp.concatenate([jnp.where(half, qt, 0.0), jnp.where(half, 0.0, qt)],
                                   axis=1).astype(BF16)

    def probs(n):
        qi, kj = qi_ref[n], kj_ref[n]
        k = k_ref[pl.ds(pl.multiple_of(kj * t, t), t), :]
        b = bias_ref[qi - kj]
        s = jnp.dot(k, q2t_s[qi], preferred_element_type=F32)
        s = s + jnp.concatenate([b, b], axis=1)
        m_blk = jnp.max(s, axis=0, keepdims=True)
        m_s[n] = m_blk
        return jnp.exp2(s - m_blk).astype(BF16)

    def values(n, p):
        acc_s[n] = jnp.dot(vt_s[kj_ref[n]], p, preferred_element_type=F32)

    def body(it, carry):
        n = it * u
        nxt = tuple(probs(n + u + c) for c in range(u))
        for c in range(u):
            values(n + c, carry[c])
        return nxt

    carry = lax.fori_loop(0, n_pairs // u - 1, body, tuple(probs(c) for c in range(u)))
    for c in range(u):
        values(n_pairs - u + c, carry[c])

    lam = lam_ref[0]
    for qi in range(nq):
        first = qi * (qi + 1) // 2
        ms = [m_s[first + j] for j in range(qi + 1)]
        m_all = functools.reduce(jnp.maximum, ms)
        acc = jnp.zeros((LANES + ONES_ROWS, 2 * t), F32)
        for j in range(qi + 1):
            acc = acc + jnp.exp2(ms[j] - m_all) * acc_s[first + j]
        o = acc[:LANES] / acc[LANES:LANES + 1]
        out = o[:, :t] - lam * o[:, t:]
        out = out * lax.rsqrt(jnp.mean(out * out, axis=0, keepdims=True) + EPS)
        out = (out * gain_ref[...] * out_scale).T
        rows = pl.ds(qi * t, t)
        o_ref[rows, :] = (out * _silu(g_ref[rows, :].astype(F32))).astype(BF16)


def _diff_attn(p4, bias, lam, gain, lam_init):
    _, b, s, _ = p4.shape
    nq = s // DIFF_T
    pairs = _diff_pairs(nq)
    qi_tab = jnp.asarray([p[0] for p in pairs], jnp.int32)
    kj_tab = jnp.asarray([p[1] for p in pairs], jnp.int32)

    def col(first):
        return pl.BlockSpec((None, None, s, LANES), lambda bi, h: (first + h, bi, 0, 0))

    smem = pl.BlockSpec(memory_space=pltpu.SMEM)
    kern = functools.partial(_diff_attn_kernel, out_scale=1.0 - lam_init)
    return pl.pallas_call(
        kern,
        grid=(b, A_HEADS),
        in_specs=[
            smem, smem, smem,
            col(COL_AQ), col(COL_AK), col(COL_AV), col(COL_AG),
            pl.BlockSpec((nq, None, DIFF_T, DIFF_T), lambda bi, h: (0, h, 0, 0)),
            pl.BlockSpec((LANES, 1), lambda bi, h: (0, 0)),
        ],
        out_specs=pl.BlockSpec((None, s, LANES), lambda bi, h: (bi, 0, h)),
        out_shape=jax.ShapeDtypeStruct((b, s, D_A), BF16),
        scratch_shapes=[
            pltpu.VMEM((nq, LANES + ONES_ROWS, DIFF_T), BF16),
            pltpu.VMEM((nq, LANES, 2 * DIFF_T), BF16),
            pltpu.VMEM((len(pairs), LANES + ONES_ROWS, 2 * DIFF_T), F32),
            pltpu.VMEM((len(pairs), 1, 2 * DIFF_T), F32),
        ],
        compiler_params=_params(("parallel", "parallel")),
        name="diff_attn",
    )(lam, qi_tab, kj_tab, p4, p4, p4, p4, bias, gain)


DIL_UNROLL = 8
STEP = 4


def _dil_variants(seq):
    out = []
    for p, dil in enumerate(DILATIONS):
        per_residue = seq // dil // BLOCK
        for _ in range(dil):
            out += [2 * p + (1 if j == 0 else 0) for j in range(per_residue)]
    return out


def _dil_split(seq, p):
    out = []
    prev_len, length = seq // DILATIONS[p - 1], seq // DILATIONS[p]
    for r_prev in range(DILATIONS[p - 1]):
        for b in range(STEP):
            r = r_prev + DILATIONS[p - 1] * b
            out.append((r * length, pl.ds(r_prev * prev_len + b, length, stride=STEP)))
    return out


def _dil_attn_kernel(var_ref, q_ref, k_ref, v_ref, g_ref, bias_ref, o_ref,
                     f32_a, f32_b, qa, ka, va, acc_s, den_s, mst, out_s, *, seq):
    n_pat = len(DILATIONS)
    n_blocks = n_pat * seq // BLOCK
    u = DIL_UNROLL
    assert n_blocks % u == 0 and n_blocks >= 2 * u

    ka[pl.ds(0, BLOCK), :] = jnp.zeros((BLOCK, LANES), BF16)
    va[pl.ds(0, BLOCK), :] = jnp.zeros((BLOCK, 2 * LANES), BF16)
    va[pl.ds(BLOCK, n_pat * seq), pl.ds(LANES, LANES)] = jnp.ones((n_pat * seq, LANES), BF16)
    for src, dst, front in ((q_ref, qa, 0), (k_ref, ka, BLOCK), (v_ref, va, BLOCK)):
        dst[pl.ds(front, seq), pl.ds(0, LANES)] = src[...]
        cur, nxt = f32_a, f32_b
        cur[...] = src[...].astype(F32)
        for p in range(1, n_pat):
            for row, piece in _dil_split(seq, p):
                x = cur[piece, :]
                if p + 1 < n_pat:
                    nxt[pl.ds(row, seq // DILATIONS[p]), :] = x
                dst[pl.ds(front + p * seq + row, seq // DILATIONS[p]), pl.ds(0, LANES)] = (
                    x.astype(BF16))
            cur, nxt = nxt, cur

    def probs(g):
        base = pl.multiple_of(g * BLOCK, BLOCK)
        s = lax.dot_general(qa[pl.ds(base, BLOCK), :], ka[pl.ds(base, 2 * BLOCK), :],
                            (((1,), (1,)), ((), ())), preferred_element_type=F32)
        s = s + bias_ref[var_ref[g]]
        m = jnp.max(s, axis=1, keepdims=True)
        mst[pl.ds(base, BLOCK), :] = jnp.broadcast_to(m, (BLOCK, LANES))
        return jnp.exp2(s - m).astype(BF16)

    def values(g, p):
        base = pl.multiple_of(g * BLOCK, BLOCK)
        pv = jnp.dot(p, va[pl.ds(base, 2 * BLOCK), :], preferred_element_type=F32)
        acc_s[pl.ds(base, BLOCK), :] = pv[:, :LANES]
        den_s[pl.ds(base, BLOCK), :] = pv[:, LANES:]

    def body(it, carry):
        g = it * u
        nxt = tuple(probs(g + u + c) for c in range(u))
        for c in range(u):
            values(g + c, carry[c])
        return nxt

    carry = lax.fori_loop(0, n_blocks // u - 1, body, tuple(probs(c) for c in range(u)))
    for c in range(u):
        values(n_blocks - u + c, carry[c])

    for p in range(n_pat - 1, 0, -1):
        length = seq // DILATIONS[p]
        for row, piece in _dil_split(seq, p):
            for k in range(length // BLOCK):
                fine = pl.ds(p * seq + row + k * BLOCK, BLOCK)
                coarse = pl.ds((p - 1) * seq + piece.start + STEP * BLOCK * k, BLOCK, stride=STEP)
                m_a, m_b = mst[coarse, :], mst[fine, :]
                m_ab = jnp.maximum(m_a, m_b)
                w_a, w_b = jnp.exp2(m_a - m_ab), jnp.exp2(m_b - m_ab)
                acc = w_a * acc_s[coarse, :] + w_b * acc_s[fine, :]
                den = w_a * den_s[coarse, :] + w_b * den_s[fine, :]
                if p > 1:
                    mst[coarse, :] = m_ab
                    acc_s[coarse, :] = acc
                    den_s[coarse, :] = den
                else:
                    out_s[coarse, :] = acc / den
    o_ref[...] = (out_s[...] * _silu(g_ref[...].astype(F32))).astype(BF16)


def _dil_attn(p4, bias):
    _, b, s, _ = p4.shape
    assert all(b == a * STEP for a, b in zip(DILATIONS, DILATIONS[1:])) and DILATIONS[0] == 1
    n_pat = len(DILATIONS)
    variants = jnp.asarray(_dil_variants(s), jnp.int32)

    def col(first):
        return pl.BlockSpec((None, None, s, LANES), lambda bi, h: (first + h, bi, 0, 0))

    rows = n_pat * s
    return pl.pallas_call(
        functools.partial(_dil_attn_kernel, seq=s),
        grid=(b, B_HEADS),
        in_specs=[
            pl.BlockSpec(memory_space=pltpu.SMEM),
            col(COL_BQ), col(COL_BK), col(COL_BV), col(COL_BG),
            pl.BlockSpec((2 * n_pat, None, BLOCK, 2 * BLOCK), lambda bi, h: (0, h, 0, 0)),
        ],
        out_specs=pl.BlockSpec((None, s, LANES), lambda bi, h: (bi, 0, h)),
        out_shape=jax.ShapeDtypeStruct((b, s, D_B), BF16),
        scratch_shapes=[
            pltpu.VMEM((s, LANES), F32),
            pltpu.VMEM((s, LANES), F32),
            pltpu.VMEM((rows, LANES), BF16),
            pltpu.VMEM((BLOCK + rows, LANES), BF16),
            pltpu.VMEM((BLOCK + rows, 2 * LANES), BF16),
            pltpu.VMEM((rows, LANES), F32),
            pltpu.VMEM((rows, LANES), F32),
            pltpu.VMEM((rows, LANES), F32),
            pltpu.VMEM((s, LANES), F32),
        ],
        compiler_params=_params(("parallel", "parallel")),
        name="dil_attn",
    )(variants, p4, p4, p4, p4, bias)


CONV_ROWS = 256


def _conv_kernel(u_ref, glu_ref, g_ref, dw_ref, db_ref, lng_ref, lnb_ref, pw_ref, o_ref,
                 hpad, hc, *, seq):
    for c in range(C_BLOCKS):
        hpad[c, pl.ds(0, CONV_PAD), :] = jnp.zeros((CONV_PAD, LANES), F32)
        u = u_ref[c].astype(F32)
        gate = glu_ref[c].astype(F32)
        hpad[c, pl.ds(CONV_PAD, seq), :] = u * (1.0 / (1.0 + jnp.exp(-gate)))

    first = CONV_PAD - (CONV_WIDTH - 1)
    for c in range(C_BLOCKS):
        cols = slice(c * LANES, (c + 1) * LANES)

        def conv_chunk(ci, carry, c=c, cols=cols):
            r0 = pl.multiple_of(ci * CONV_ROWS, CONV_ROWS)
            acc = jnp.broadcast_to(db_ref[:, cols], (CONV_ROWS, LANES))
            for j in range(CONV_WIDTH):
                tap = hpad[c, pl.ds(r0 + (first + j), CONV_ROWS), :]
                acc = acc + dw_ref[pl.ds(j, 1), cols] * tap
            hc[pl.ds(r0, CONV_ROWS), cols] = acc
            return carry

        lax.fori_loop(0, seq // CONV_ROWS, conv_chunk, 0)

    def tail(ci, carry):
        r0 = pl.multiple_of(ci * CONV_ROWS, CONV_ROWS)
        sl = pl.ds(r0, CONV_ROWS)
        x = hc[sl, :]
        mu = jnp.mean(x, axis=-1, keepdims=True)
        xc = x - mu
        var = jnp.mean(xc * xc, axis=-1, keepdims=True)
        y = xc * lax.rsqrt(var + EPS) * lng_ref[...] + lnb_ref[...]
        y = _silu(y)
        z = jnp.dot(y.astype(BF16), pw_ref[...], preferred_element_type=F32)
        for c in range(C_BLOCKS):
            cols = slice(c * LANES, (c + 1) * LANES)
            o_ref[sl, cols] = (z[:, cols] * _silu(g_ref[c, sl, :].astype(F32))).astype(BF16)
        return carry

    lax.fori_loop(0, seq // CONV_ROWS, tail, 0)


def _conv_mixer(p4, conv_dw, conv_b, ln_g, ln_b, pw_bf16):
    _, b, s, _ = p4.shape

    def cols(first):
        return lambda bi: (first // C_BLOCKS, bi, 0, 0)

    def whole(shape):
        return pl.BlockSpec(shape, lambda bi: (0,) * len(shape))

    blocks = (C_BLOCKS, None, s, LANES)
    return pl.pallas_call(
        functools.partial(_conv_kernel, seq=s),
        grid=(b,),
        in_specs=[
            pl.BlockSpec(blocks, cols(COL_CU)),
            pl.BlockSpec(blocks, cols(COL_CGLU)),
            pl.BlockSpec(blocks, cols(COL_CG)),
            whole((CONV_WIDTH, D_C)),
            whole((1, D_C)),
            whole((1, D_C)),
            whole((1, D_C)),
            whole((D_C, D_C)),
        ],
        out_specs=pl.BlockSpec((None, s, D_C), lambda bi: (bi, 0, 0)),
        out_shape=jax.ShapeDtypeStruct((b, s, D_C), BF16),
        scratch_shapes=[pltpu.VMEM((C_BLOCKS, CONV_PAD + s, LANES), F32),
                        pltpu.VMEM((s, D_C), F32)],
        compiler_params=_params(("parallel",)),
        name="conv_mixer",
    )(p4, p4, p4, conv_dw, conv_b.reshape(1, D_C), ln_g.reshape(1, D_C), ln_b.reshape(1, D_C),
      pw_bf16)


OUT_TM = 512
OUT_ROWS = 256


def _out_proj_kernel(ya_ref, yb_ref, yc_ref, x_ref, w_ref, g_ref, b_ref, o_ref, *, alpha):
    for c in range(OUT_TM // OUT_ROWS):
        rows = pl.ds(c * OUT_ROWS, OUT_ROWS)
        y = jnp.dot(ya_ref[rows, :], w_ref[pl.ds(0, D_A), :], preferred_element_type=F32)
        y = y + jnp.dot(yb_ref[rows, :], w_ref[pl.ds(D_A, D_B), :], preferred_element_type=F32)
        y = y + jnp.dot(yc_ref[rows, :], w_ref[pl.ds(D_A + D_B, D_C), :],
                        preferred_element_type=F32)
        z = alpha * x_ref[rows, :] + y
        mu = jnp.mean(z, axis=-1, keepdims=True)
        zc = z - mu
        var = jnp.mean(zc * zc, axis=-1, keepdims=True)
        o_ref[rows, :] = zc * lax.rsqrt(var + EPS) * g_ref[...] + b_ref[...]


def _out_proj(ya, yb, yc, x2d, w_bf16, ln_g, ln_b, alpha):
    m = x2d.shape[0]

    def rows(width):
        return pl.BlockSpec((OUT_TM, width), lambda i: (i, 0))

    def whole(shape):
        return pl.BlockSpec(shape, lambda i: (0, 0))

    return pl.pallas_call(
        functools.partial(_out_proj_kernel, alpha=alpha),
        grid=(m // OUT_TM,),
        in_specs=[rows(D_A), rows(D_B), rows(D_C), rows(D_MODEL),
                  whole((D_MODEL, D_MODEL)), whole((1, D_MODEL)), whole((1, D_MODEL))],
        out_specs=rows(D_MODEL),
        out_shape=jax.ShapeDtypeStruct((m, D_MODEL), F32),
        compiler_params=_params(("parallel",)),
        name="out_proj",
    )(ya, yb, yc, x2d, w_bf16, ln_g.reshape(1, D_MODEL), ln_b.reshape(1, D_MODEL))


def _in_col_scale():
    scale = jnp.ones((D_IN,), F32)
    scale = scale.at[COL_AQ * LANES:COL_AQ * LANES + D_A].set((LANES // 2) ** -0.5 * LOG2E)
    return scale.at[COL_BQ * LANES:COL_BQ * LANES + D_B].set(LANES ** -0.5 * LOG2E)


def _layer(x2d, batch, seq, layer, depth, w_in, diff_lambda, diff_head_gain, conv_dw, conv_b,
           conv_ln_g, conv_ln_b, conv_pw, w_out, ln_g, ln_b, diff_bias, dil_bias):
    m = batch * seq
    lam_init = 0.8 - 0.6 * math.exp(-0.3 * layer)
    alpha = (2 * depth) ** 0.25
    proj = _in_proj(x2d, (w_in * _in_col_scale()).astype(BF16))
    p4 = proj.reshape(N_COL_BLOCKS, batch, seq, LANES)

    lam_v = diff_lambda.astype(F32)
    lam = (jnp.exp(jnp.sum(lam_v[0] * lam_v[1])) - jnp.exp(jnp.sum(lam_v[2] * lam_v[3]))
           + lam_init).reshape(1)
    ya = _diff_attn(p4, diff_bias, lam, diff_head_gain.astype(F32).reshape(LANES, 1), lam_init)
    yb = _dil_attn(p4, dil_bias)
    yc = _conv_mixer(p4, conv_dw, conv_b, conv_ln_g, conv_ln_b, conv_pw.astype(BF16))
    return _out_proj(ya.reshape(m, D_A), yb.reshape(m, D_B), yc.reshape(m, D_C), x2d,
                     w_out.astype(BF16), ln_g, ln_b, alpha)


def kernel(x, w_in, diff_lambda, diff_head_gain, conv_dw, conv_b, conv_ln_g, conv_ln_b, conv_pw,
           w_out, ln_g, ln_b, rel_bias):
    batch, seq, _ = x.shape
    depth = w_in.shape[0]
    diff_bias = _diff_bias_tiles(rel_bias, seq)
    dil_bias = _dil_bias_tiles(rel_bias)
    x2d = x.reshape(batch * seq, D_MODEL)
    for layer in range(depth):
        x2d = _layer(x2d, batch, seq, layer, depth, w_in[layer], diff_lambda[layer],
                     diff_head_gain[layer], conv_dw[layer], conv_b[layer], conv_ln_g[layer],
                     conv_ln_b[layer], conv_pw[layer], w_out[layer], ln_g[layer], ln_b[layer],
                     diff_bias, dil_bias)
    return x2d.reshape(batch, seq, D_MODEL)
```

```python
import functools
import math

import jax
import jax.numpy as jnp
from jax import lax
from jax.experimental import pallas as pl
from jax.experimental.pallas import tpu as pltpu

F32 = jnp.float32
BF16 = jnp.bfloat16

LANES = 128
D_MODEL = 2048
A_HEADS = 4
B_HEADS = 8
D_A = A_HEADS * LANES
D_B = B_HEADS * LANES
D_C = D_MODEL - D_A - D_B
C_BLOCKS = D_C // LANES
CONV_WIDTH = 31
CONV_PAD = 32
DILATIONS = (1, 4, 16)
BLOCK = 128
N_BUCKETS = 32
MAX_DISTANCE = 2048
EPS = 1e-5
LOG2E = math.log2(math.e)
D_IN = 4 * D_A + 4 * D_B + 3 * D_C
N_COL_BLOCKS = D_IN // LANES
COL_AQ, COL_AK, COL_AV, COL_AG = 0, 4, 8, 12
COL_BQ, COL_BK, COL_BV, COL_BG = 16, 24, 32, 40
COL_CU, COL_CGLU, COL_CG = 48, 52, 56

VMEM_LIMIT_BYTES = 56 * 1024 * 1024


def _params(semantics):
    return pltpu.CompilerParams(dimension_semantics=semantics, vmem_limit_bytes=VMEM_LIMIT_BYTES)


def _silu(x):
    return x * (1.0 / (1.0 + jnp.exp(-x)))


IN_TM = 1024
IN_TN = 1280


def _in_proj_kernel(x_ref, w_ref, o_ref, xb_ref):
    @pl.when(pl.program_id(1) == 0)
    def _():
        xb_ref[...] = x_ref[...].astype(BF16)

    acc = jnp.dot(xb_ref[...], w_ref[...], preferred_element_type=F32)
    for c in range(IN_TN // LANES):
        o_ref[c] = acc[:, c * LANES:(c + 1) * LANES].astype(BF16)


def _in_proj(x2d, w_bf16):
    m = x2d.shape[0]
    nb = IN_TN // LANES
    return pl.pallas_call(
        _in_proj_kernel,
        grid=(m // IN_TM, D_IN // IN_TN),
        in_specs=[
            pl.BlockSpec((IN_TM, D_MODEL), lambda i, j: (i, 0)),
            pl.BlockSpec((D_MODEL, IN_TN), lambda i, j: (0, j)),
        ],
        out_specs=pl.BlockSpec((nb, IN_TM, LANES), lambda i, j: (j, i, 0)),
        out_shape=jax.ShapeDtypeStruct((N_COL_BLOCKS, m, LANES), BF16),
        scratch_shapes=[pltpu.VMEM((IN_TM, D_MODEL), BF16)],
        compiler_params=_params(("parallel", "arbitrary")),
        name="in_proj",
    )(x2d, w_bf16)


def _t5_bucket(dist):
    max_exact = N_BUCKETS // 2
    d = jnp.maximum(dist, 0)
    df = jnp.maximum(d, 1).astype(F32)
    large = max_exact + (jnp.log(df / max_exact) / math.log(MAX_DISTANCE / max_exact)
                         * (N_BUCKETS - max_exact)).astype(jnp.int32)
    large = jnp.minimum(large, N_BUCKETS - 1)
    return jnp.where(d < max_exact, d, large)


DIFF_T = 256


def _bucket_lookup(table, bucket):
    onehot = bucket[..., None] == jnp.arange(N_BUCKETS)
    cols = [jnp.sum(jnp.where(onehot, table[:, h].astype(F32), 0.0), axis=-1)
            for h in range(table.shape[1])]
    return jnp.stack(cols)


def _diff_bias_tiles(rel_bias, seq):
    nd = seq // DIFF_T
    key = jnp.arange(DIFF_T)[:, None]
    qry = jnp.arange(DIFF_T)[None, :]
    dist = jnp.arange(nd)[:, None, None] * DIFF_T + qry - key
    tiles = _bucket_lookup(rel_bias[:, :A_HEADS], _t5_bucket(jnp.clip(dist, 0, seq - 1)))
    tiles = jnp.where(dist[None] >= 0, tiles * LOG2E, -jnp.inf)
    return jnp.transpose(tiles, (1, 0, 2, 3))


def _dil_bias_tiles(rel_bias):
    qi = jnp.arange(BLOCK)[:, None]
    kj = jnp.arange(2 * BLOCK)[None, :]
    lag = BLOCK + qi - kj
    valid = (lag >= 0) & (lag <= BLOCK)
    tiles = []
    for dil in DILATIONS:
        b = _bucket_lookup(rel_bias[:, A_HEADS:],
                           _t5_bucket(jnp.clip(lag, 0, 2 * BLOCK - 1) * dil))
        tiles.append(jnp.where(valid[None], b * LOG2E, -jnp.inf))
        tiles.append(jnp.where((valid & (kj >= BLOCK))[None], b * LOG2E, -jnp.inf))
    return jnp.stack(tiles)


DIFF_UNROLL = 2
ONES_ROWS = 16


def _diff_pairs(nq):
    return [(qi, kj) for qi in range(nq) for kj in range(qi + 1)]


def _diff_attn_kernel(lam_ref, qi_ref, kj_ref, q_ref, k_ref, v_ref, g_ref, bias_ref, gain_ref,
                      o_ref, vt_s, q2t_s, acc_s, m_s, s_scr, p_scr, *, out_scale):
    t = DIFF_T
    nq = q_ref.shape[0] // t
    n_pairs = len(_diff_pairs(nq))
    u = DIFF_UNROLL
    assert n_pairs % u == 0 and n_pairs >= 2 * u

    half = lax.broadcasted_iota(jnp.int32, (LANES, t), 0) < LANES // 2
    ones = jnp.ones((ONES_ROWS, t), BF16)
    for j in range(nq):
        rows = pl.ds(j * t, t)
        vt_s[j] = jnp.concatenate([v_ref[rows, :].astype(F32).T.astype(BF16), ones], axis=0)
        qt = q_ref[rows, :].astype(F32).T
        q2t_s[j] = jnp.concatenate([jnp.where(half, qt, 0.0), jnp.where(half, 0.0, qt)],
                                   axis=1).astype(BF16)

    n_groups = n_pairs // u

    def scores(grp, slot):
        for c in range(u):
            n = grp * u + c
            k = k_ref[pl.ds(pl.multiple_of(kj_ref[n] * t, t), t), :]
            s_scr[slot, c] = jnp.dot(k, q2t_s[qi_ref[n]], preferred_element_type=F32)

    def probs(grp, slot):
        for c in range(u):
            n = grp * u + c
            b = bias_ref[qi_ref[n] - kj_ref[n]]
            s = s_scr[slot, c] + jnp.concatenate([b, b], axis=1)
            m_blk = jnp.max(s, axis=0, keepdims=True)
            m_s[n] = m_blk
            p_scr[slot, c] = jnp.exp2(s - m_blk).astype(BF16)

    def values(grp, slot):
        for c in range(u):
            n = grp * u + c
            acc_s[n] = jnp.dot(vt_s[kj_ref[n]], p_scr[slot, c], preferred_element_type=F32)

    def trip(g, parity):
        values(g - 2, parity)
        probs(g - 1, 1 - parity)
        scores(g, parity)

    scores(0, 0)
    probs(0, 0)
    scores(1, 1)

    def body(it, carry):
        g = 2 * it + 2
        trip(g, 0)
        trip(g + 1, 1)
        return carry

    assert n_groups % 2 == 0
    lax.fori_loop(0, (n_groups - 2) // 2, body, 0)
    values(n_groups - 2, 0)
    probs(n_groups - 1, 1)
    values(n_groups - 1, 1)

    lam = lam_ref[0]
    for qi in range(nq):
        first = qi * (qi + 1) // 2
        ms = [m_s[first + j] for j in range(qi + 1)]
        m_all = functools.reduce(jnp.maximum, ms)
        acc = jnp.zeros((LANES + ONES_ROWS, 2 * t), F32)
        for j in range(qi + 1):
            acc = acc + jnp.exp2(ms[j] - m_all) * acc_s[first + j]
        o = acc[:LANES] / acc[LANES:LANES + 1]
        out = o[:, :t] - lam * o[:, t:]
        out = out * lax.rsqrt(jnp.mean(out * out, axis=0, keepdims=True) + EPS)
        out = (out * gain_ref[...] * out_scale).T
        rows = pl.ds(qi * t, t)
        o_ref[rows, :] = (out * _silu(g_ref[rows, :].astype(F32))).astype(BF16)


def _diff_attn(p4, bias, lam, gain, lam_init):
    _, b, s, _ = p4.shape
    nq = s // DIFF_T
    pairs = _diff_pairs(nq)
    qi_tab = jnp.asarray([p[0] for p in pairs], jnp.int32)
    kj_tab = jnp.asarray([p[1] for p in pairs], jnp.int32)

    def col(first):
        return pl.BlockSpec((None, None, s, LANES), lambda bi, h: (first + h, bi, 0, 0))

    smem = pl.BlockSpec(memory_space=pltpu.SMEM)
    kern = functools.partial(_diff_attn_kernel, out_scale=1.0 - lam_init)
    return pl.pallas_call(
        kern,
        grid=(b, A_HEADS),
        in_specs=[
            smem, smem, smem,
            col(COL_AQ), col(COL_AK), col(COL_AV), col(COL_AG),
            pl.BlockSpec((nq, None, DIFF_T, DIFF_T), lambda bi, h: (0, h, 0, 0)),
            pl.BlockSpec((LANES, 1), lambda bi, h: (0, 0)),
        ],
        out_specs=pl.BlockSpec((None, s, LANES), lambda bi, h: (bi, 0, h)),
        out_shape=jax.ShapeDtypeStruct((b, s, D_A), BF16),
        scratch_shapes=[
            pltpu.VMEM((nq, LANES + ONES_ROWS, DIFF_T), BF16),
            pltpu.VMEM((nq, LANES, 2 * DIFF_T), BF16),
            pltpu.VMEM((len(pairs), LANES + ONES_ROWS, 2 * DIFF_T), F32),
            pltpu.VMEM((len(pairs), 1, 2 * DIFF_T), F32),
            pltpu.VMEM((2, DIFF_UNROLL, DIFF_T, 2 * DIFF_T), F32),
            pltpu.VMEM((2, DIFF_UNROLL, DIFF_T, 2 * DIFF_T), BF16),
        ],
        compiler_params=_params(("parallel", "parallel")),
        name="diff_attn",
    )(lam, qi_tab, kj_tab, p4, p4, p4, p4, bias, gain)


DIL_UNROLL = 8
STEP = 4


def _dil_variants(seq):
    out = []
    for p, dil in enumerate(DILATIONS):
        per_residue = seq // dil // BLOCK
        for _ in range(dil):
            out += [2 * p + (1 if j == 0 else 0) for j in range(per_residue)]
    return out


def _dil_split(seq, p):
    out = []
    prev_len, length = seq // DILATIONS[p - 1], seq // DILATIONS[p]
    for r_prev in range(DILATIONS[p - 1]):
        for b in range(STEP):
            r = r_prev + DILATIONS[p - 1] * b
            out.append((r * length, pl.ds(r_prev * prev_len + b, length, stride=STEP)))
    return out


def _dil_attn_kernel(var_ref, q_ref, k_ref, v_ref, g_ref, bias_ref, o_ref,
                     f32_a, f32_b, qa, ka, va, acc_s, den_s, mst, out_s, s_scr, p_scr, *, seq):
    n_pat = len(DILATIONS)
    n_blocks = n_pat * seq // BLOCK
    u = DIL_UNROLL
    assert n_blocks % u == 0 and n_blocks >= 2 * u

    ka[pl.ds(0, BLOCK), :] = jnp.zeros((BLOCK, LANES), BF16)
    va[pl.ds(0, BLOCK), :] = jnp.zeros((BLOCK, 2 * LANES), BF16)
    va[pl.ds(BLOCK, n_pat * seq), pl.ds(LANES, LANES)] = jnp.ones((n_pat * seq, LANES), BF16)
    for src, dst, front in ((q_ref, qa, 0), (k_ref, ka, BLOCK), (v_ref, va, BLOCK)):
        dst[pl.ds(front, seq), pl.ds(0, LANES)] = src[...]
        cur, nxt = f32_a, f32_b
        cur[...] = src[...].astype(F32)
        for p in range(1, n_pat):
            for row, piece in _dil_split(seq, p):
                x = cur[piece, :]
                if p + 1 < n_pat:
                    nxt[pl.ds(row, seq // DILATIONS[p]), :] = x
                dst[pl.ds(front + p * seq + row, seq // DILATIONS[p]), pl.ds(0, LANES)] = (
                    x.astype(BF16))
            cur, nxt = nxt, cur

    n_groups = n_blocks // u

    def scores(grp, slot):
        for c in range(u):
            base = pl.multiple_of((grp * u + c) * BLOCK, BLOCK)
            s_scr[slot, c] = lax.dot_general(
                qa[pl.ds(base, BLOCK), :], ka[pl.ds(base, 2 * BLOCK), :],
                (((1,), (1,)), ((), ())), preferred_element_type=F32)

    def probs(grp, slot):
        for c in range(u):
            g = grp * u + c
            s = s_scr[slot, c] + bias_ref[var_ref[g]]
            m = jnp.max(s, axis=1, keepdims=True)
            mst[pl.ds(pl.multiple_of(g * BLOCK, BLOCK), BLOCK), :] = jnp.broadcast_to(
                m, (BLOCK, LANES))
            p_scr[slot, c] = jnp.exp2(s - m).astype(BF16)

    def values(grp, slot):
        for c in range(u):
            base = pl.multiple_of((grp * u + c) * BLOCK, BLOCK)
            pv = jnp.dot(p_scr[slot, c], va[pl.ds(base, 2 * BLOCK), :],
                         preferred_element_type=F32)
            acc_s[pl.ds(base, BLOCK), :] = pv[:, :LANES]
            den_s[pl.ds(base, BLOCK), :] = pv[:, LANES:]

    def trip(g, parity):
        values(g - 2, parity)
        probs(g - 1, 1 - parity)
        scores(g, parity)

    scores(0, 0)
    probs(0, 0)
    scores(1, 1)

    def body(it, carry):
        g = 2 * it + 2
        trip(g, 0)
        trip(g + 1, 1)
        return carry

    assert n_groups % 2 == 0
    lax.fori_loop(0, (n_groups - 2) // 2, body, 0)
    values(n_groups - 2, 0)
    probs(n_groups - 1, 1)
    values(n_groups - 1, 1)

    for p in range(n_pat - 1, 0, -1):
        length = seq // DILATIONS[p]
        for row, piece in _dil_split(seq, p):
            for k in range(length // BLOCK):
                fine = pl.ds(p * seq + row + k * BLOCK, BLOCK)
                coarse = pl.ds((p - 1) * seq + piece.start + STEP * BLOCK * k, BLOCK, stride=STEP)
                m_a, m_b = mst[coarse, :], mst[fine, :]
                m_ab = jnp.maximum(m_a, m_b)
                w_a, w_b = jnp.exp2(m_a - m_ab), jnp.exp2(m_b - m_ab)
                acc = w_a * acc_s[coarse, :] + w_b * acc_s[fine, :]
                den = w_a * den_s[coarse, :] + w_b * den_s[fine, :]
                if p > 1:
                    mst[coarse, :] = m_ab
                    acc_s[coarse, :] = acc
                    den_s[coarse, :] = den
                else:
                    out_s[coarse, :] = acc / den
    o_ref[...] = (out_s[...] * _silu(g_ref[...].astype(F32))).astype(BF16)


def _dil_attn(p4, bias):
    _, b, s, _ = p4.shape
    assert all(b == a * STEP for a, b in zip(DILATIONS, DILATIONS[1:])) and DILATIONS[0] == 1
    n_pat = len(DILATIONS)
    variants = jnp.asarray(_dil_variants(s), jnp.int32)

    def col(first):
        return pl.BlockSpec((None, None, s, LANES), lambda bi, h: (first + h, bi, 0, 0))

    rows = n_pat * s
    return pl.pallas_call(
        functools.partial(_dil_attn_kernel, seq=s),
        grid=(b, B_HEADS),
        in_specs=[
            pl.BlockSpec(memory_space=pltpu.SMEM),
            col(COL_BQ), col(COL_BK), col(COL_BV), col(COL_BG),
            pl.BlockSpec((2 * n_pat, None, BLOCK, 2 * BLOCK), lambda bi, h: (0, h, 0, 0)),
        ],
        out_specs=pl.BlockSpec((None, s, LANES), lambda bi, h: (bi, 0, h)),
        out_shape=jax.ShapeDtypeStruct((b, s, D_B), BF16),
        scratch_shapes=[
            pltpu.VMEM((s, LANES), F32),
            pltpu.VMEM((s, LANES), F32),
            pltpu.VMEM((rows, LANES), BF16),
            pltpu.VMEM((BLOCK + rows, LANES), BF16),
            pltpu.VMEM((BLOCK + rows, 2 * LANES), BF16),
            pltpu.VMEM((rows, LANES), F32),
            pltpu.VMEM((rows, LANES), F32),
            pltpu.VMEM((rows, LANES), F32),
            pltpu.VMEM((s, LANES), F32),
            pltpu.VMEM((2, DIL_UNROLL, BLOCK, 2 * BLOCK), F32),
            pltpu.VMEM((2, DIL_UNROLL, BLOCK, 2 * BLOCK), BF16),
        ],
        compiler_params=_params(("parallel", "parallel")),
        name="dil_attn",
    )(variants, p4, p4, p4, p4, bias)


CONV_ROWS = 256


def _conv_kernel(u_ref, glu_ref, g_ref, dw_ref, db_ref, lng_ref, lnb_ref, pw_ref, o_ref,
                 hpad, hc, *, seq):
    for c in range(C_BLOCKS):
        hpad[c, pl.ds(0, CONV_PAD), :] = jnp.zeros((CONV_PAD, LANES), F32)
        u = u_ref[c].astype(F32)
        gate = glu_ref[c].astype(F32)
        hpad[c, pl.ds(CONV_PAD, seq), :] = u * (1.0 / (1.0 + jnp.exp(-gate)))

    first = CONV_PAD - (CONV_WIDTH - 1)
    for c in range(C_BLOCKS):
        cols = slice(c * LANES, (c + 1) * LANES)

        def conv_chunk(ci, carry, c=c, cols=cols):
            r0 = pl.multiple_of(ci * CONV_ROWS, CONV_ROWS)
            acc = jnp.broadcast_to(db_ref[:, cols], (CONV_ROWS, LANES))
            for j in range(CONV_WIDTH):
                tap = hpad[c, pl.ds(r0 + (first + j), CONV_ROWS), :]
                acc = acc + dw_ref[pl.ds(j, 1), cols] * tap
            hc[pl.ds(r0, CONV_ROWS), cols] = acc
            return carry

        lax.fori_loop(0, seq // CONV_ROWS, conv_chunk, 0)

    def tail(ci, carry):
        r0 = pl.multiple_of(ci * CONV_ROWS, CONV_ROWS)
        sl = pl.ds(r0, CONV_ROWS)
        x = hc[sl, :]
        mu = jnp.mean(x, axis=-1, keepdims=True)
        xc = x - mu
        var = jnp.mean(xc * xc, axis=-1, keepdims=True)
        y = xc * lax.rsqrt(var + EPS) * lng_ref[...] + lnb_ref[...]
        y = _silu(y)
        z = jnp.dot(y.astype(BF16), pw_ref[...], preferred_element_type=F32)
        for c in range(C_BLOCKS):
            cols = slice(c * LANES, (c + 1) * LANES)
            o_ref[sl, cols] = (z[:, cols] * _silu(g_ref[c, sl, :].astype(F32))).astype(BF16)
        return carry

    lax.fori_loop(0, seq // CONV_ROWS, tail, 0)


def _conv_mixer(p4, conv_dw, conv_b, ln_g, ln_b, pw_bf16):
    _, b, s, _ = p4.shape

    def cols(first):
        return lambda bi: (first // C_BLOCKS, bi, 0, 0)

    def whole(shape):
        return pl.BlockSpec(shape, lambda bi: (0,) * len(shape))

    blocks = (C_BLOCKS, None, s, LANES)
    return pl.pallas_call(
        functools.partial(_conv_kernel, seq=s),
        grid=(b,),
        in_specs=[
            pl.BlockSpec(blocks, cols(COL_CU)),
            pl.BlockSpec(blocks, cols(COL_CGLU)),
            pl.BlockSpec(blocks, cols(COL_CG)),
            whole((CONV_WIDTH, D_C)),
            whole((1, D_C)),
            whole((1, D_C)),
            whole((1, D_C)),
            whole((D_C, D_C)),
        ],
        out_specs=pl.BlockSpec((None, s, D_C), lambda bi: (bi, 0, 0)),
        out_shape=jax.ShapeDtypeStruct((b, s, D_C), BF16),
        scratch_shapes=[pltpu.VMEM((C_BLOCKS, CONV_PAD + s, LANES), F32),
                        pltpu.VMEM((s, D_C), F32)],
        compiler_params=_params(("parallel",)),
        name="conv_mixer",
    )(p4, p4, p4, conv_dw, conv_b.reshape(1, D_C), ln_g.reshape(1, D_C), ln_b.reshape(1, D_C),
      pw_bf16)


OUT_TM = 512
OUT_ROWS = 256


def _out_proj_kernel(ya_ref, yb_ref, yc_ref, x_ref, w_ref, g_ref, b_ref, o_ref, *, alpha):
    for c in range(OUT_TM // OUT_ROWS):
        rows = pl.ds(c * OUT_ROWS, OUT_ROWS)
        y = jnp.dot(ya_ref[rows, :], w_ref[pl.ds(0, D_A), :], preferred_element_type=F32)
        y = y + jnp.dot(yb_ref[rows, :], w_ref[pl.ds(D_A, D_B), :], preferred_element_type=F32)
        y = y + jnp.dot(yc_ref[rows, :], w_ref[pl.ds(D_A + D_B, D_C), :],
                        preferred_element_type=F32)
        z = alpha * x_ref[rows, :] + y
        mu = jnp.mean(z, axis=-1, keepdims=True)
        zc = z - mu
        var = jnp.mean(zc * zc, axis=-1, keepdims=True)
        o_ref[rows, :] = zc * lax.rsqrt(var + EPS) * g_ref[...] + b_ref[...]


def _out_proj(ya, yb, yc, x2d, w_bf16, ln_g, ln_b, alpha):
    m = x2d.shape[0]

    def rows(width):
        return pl.BlockSpec((OUT_TM, width), lambda i: (i, 0))

    def whole(shape):
        return pl.BlockSpec(shape, lambda i: (0, 0))

    return pl.pallas_call(
        functools.partial(_out_proj_kernel, alpha=alpha),
        grid=(m // OUT_TM,),
        in_specs=[rows(D_A), rows(D_B), rows(D_C), rows(D_MODEL),
                  whole((D_MODEL, D_MODEL)), whole((1, D_MODEL)), whole((1, D_MODEL))],
        out_specs=rows(D_MODEL),
        out_shape=jax.ShapeDtypeStruct((m, D_MODEL), F32),
        compiler_params=_params(("parallel",)),
        name="out_proj",
    )(ya, yb, yc, x2d, w_bf16, ln_g.reshape(1, D_MODEL), ln_b.reshape(1, D_MODEL))


def _in_col_scale():
    scale = jnp.ones((D_IN,), F32)
    scale = scale.at[COL_AQ * LANES:COL_AQ * LANES + D_A].set((LANES // 2) ** -0.5 * LOG2E)
    return scale.at[COL_BQ * LANES:COL_BQ * LANES + D_B].set(LANES ** -0.5 * LOG2E)


def _layer(x2d, batch, seq, layer, depth, w_in, diff_lambda, diff_head_gain, conv_dw, conv_b,
           conv_ln_g, conv_ln_b, conv_pw, w_out, ln_g, ln_b, diff_bias, dil_bias):
    m = batch * seq
    lam_init = 0.8 - 0.6 * math.exp(-0.3 * layer)
    alpha = (2 * depth) ** 0.25
    proj = _in_proj(x2d, (w_in * _in_col_scale()).astype(BF16))
    p4 = proj.reshape(N_COL_BLOCKS, batch, seq, LANES)

    lam_v = diff_lambda.astype(F32)
    lam = (jnp.exp(jnp.sum(lam_v[0] * lam_v[1])) - jnp.exp(jnp.sum(lam_v[2] * lam_v[3]))
           + lam_init).reshape(1)
    ya = _diff_attn(p4, diff_bias, lam, diff_head_gain.astype(F32).reshape(LANES, 1), lam_init)
    yb = _dil_attn(p4, dil_bias)
    yc = _conv_mixer(p4, conv_dw, conv_b, conv_ln_g, conv_ln_b, conv_pw.astype(BF16))
    return _out_proj(ya.reshape(m, D_A), yb.reshape(m, D_B), yc.reshape(m, D_C), x2d,
                     w_out.astype(BF16), ln_g, ln_b, alpha)


def kernel(x, w_in, diff_lambda, diff_head_gain, conv_dw, conv_b, conv_ln_g, conv_ln_b, conv_pw,
           w_out, ln_g, ln_b, rel_bias):
    batch, seq, _ = x.shape
    depth = w_in.shape[0]
    diff_bias = _diff_bias_tiles(rel_bias, seq)
    dil_bias = _dil_bias_tiles(rel_bias)
    x2d = x.reshape(batch * seq, D_MODEL)
    for layer in range(depth):
        x2d = _layer(x2d, batch, seq, layer, depth, w_in[layer], diff_lambda[layer],
                     diff_head_gain[layer], conv_dw[layer], conv_b[layer], conv_ln_g[layer],
                     conv_ln_b[layer], conv_pw[layer], w_out[layer], ln_g[layer], ln_b[layer],
                     diff_bias, dil_bias)
    return x2d.reshape(batch, seq, D_MODEL)
```

```python
import functools
import math

import jax
import jax.numpy as jnp
from jax import lax
from jax.experimental import pallas as pl
from jax.experimental.pallas import tpu as pltpu

F32 = jnp.float32
BF16 = jnp.bfloat16

LANES = 128
D_MODEL = 2048
A_HEADS = 4
B_HEADS = 8
D_A = A_HEADS * LANES
D_B = B_HEADS * LANES
D_C = D_MODEL - D_A - D_B
C_BLOCKS = D_C // LANES
CONV_WIDTH = 31
CONV_PAD = 32
DILATIONS = (1, 4, 16)
BLOCK = 128
N_BUCKETS = 32
MAX_DISTANCE = 2048
EPS = 1e-5
LOG2E = math.log2(math.e)
D_IN = 4 * D_A + 4 * D_B + 3 * D_C
N_COL_BLOCKS = D_IN // LANES
COL_AQ, COL_AK, COL_AV, COL_AG = 0, 4, 8, 12
COL_BQ, COL_BK, COL_BV, COL_BG = 16, 24, 32, 40
COL_CU, COL_CGLU, COL_CG = 48, 52, 56

VMEM_LIMIT_BYTES = 56 * 1024 * 1024


def _params(semantics):
    return pltpu.CompilerParams(dimension_semantics=semantics, vmem_limit_bytes=VMEM_LIMIT_BYTES)


def _silu(x):
    return x * (1.0 / (1.0 + jnp.exp(-x)))


IN_TM = 1024
IN_TN = 1280


def _in_proj_kernel(x_ref, w_ref, o_ref, xb_ref):
    @pl.when(pl.program_id(1) == 0)
    def _():
        xb_ref[...] = x_ref[...].astype(BF16)

    acc = jnp.dot(xb_ref[...], w_ref[...], preferred_element_type=F32)
    for c in range(IN_TN // LANES):
        o_ref[c] = acc[:, c * LANES:(c + 1) * LANES].astype(BF16)


def _in_proj(x2d, w_bf16):
    m = x2d.shape[0]
    nb = IN_TN // LANES
    return pl.pallas_call(
        _in_proj_kernel,
        grid=(m // IN_TM, D_IN // IN_TN),
        in_specs=[
            pl.BlockSpec((IN_TM, D_MODEL), lambda i, j: (i, 0)),
            pl.BlockSpec((D_MODEL, IN_TN), lambda i, j: (0, j)),
        ],
        out_specs=pl.BlockSpec((nb, IN_TM, LANES), lambda i, j: (j, i, 0)),
        out_shape=jax.ShapeDtypeStruct((N_COL_BLOCKS, m, LANES), BF16),
        scratch_shapes=[pltpu.VMEM((IN_TM, D_MODEL), BF16)],
        compiler_params=_params(("parallel", "arbitrary")),
        name="in_proj",
    )(x2d, w_bf16)


def _t5_bucket(dist):
    max_exact = N_BUCKETS // 2
    d = jnp.maximum(dist, 0)
    df = jnp.maximum(d, 1).astype(F32)
    large = max_exact + (jnp.log(df / max_exact) / math.log(MAX_DISTANCE / max_exact)
                         * (N_BUCKETS - max_exact)).astype(jnp.int32)
    large = jnp.minimum(large, N_BUCKETS - 1)
    return jnp.where(d < max_exact, d, large)


DIFF_T = 256


def _bucket_lookup(table, bucket):
    onehot = bucket[..., None] == jnp.arange(N_BUCKETS)
    cols = [jnp.sum(jnp.where(onehot, table[:, h].astype(F32), 0.0), axis=-1)
            for h in range(table.shape[1])]
    return jnp.stack(cols)


def _diff_bias_tiles(rel_bias, seq):
    nd = seq // DIFF_T
    key = jnp.arange(DIFF_T)[:, None]
    qry = jnp.arange(DIFF_T)[None, :]
    dist = jnp.arange(nd)[:, None, None] * DIFF_T + qry - key
    tiles = _bucket_lookup(rel_bias[:, :A_HEADS], _t5_bucket(jnp.clip(dist, 0, seq - 1)))
    tiles = jnp.where(dist[None] >= 0, tiles * LOG2E, -jnp.inf)
    return jnp.transpose(tiles, (1, 0, 2, 3))


def _dil_bias_tiles(rel_bias):
    qi = jnp.arange(BLOCK)[:, None]
    kj = jnp.arange(2 * BLOCK)[None, :]
    lag = BLOCK + qi - kj
    valid = (lag >= 0) & (lag <= BLOCK)
    tiles = []
    for dil in DILATIONS:
        b = _bucket_lookup(rel_bias[:, A_HEADS:],
                           _t5_bucket(jnp.clip(lag, 0, 2 * BLOCK - 1) * dil))
        tiles.append(jnp.where(valid[None], b * LOG2E, -jnp.inf))
        tiles.append(jnp.where((valid & (kj >= BLOCK))[None], b * LOG2E, -jnp.inf))
    return jnp.stack(tiles)


DIFF_UNROLL = 3
ONES_ROWS = 16


def _diff_pairs(nq):
    return [(qi, kj) for qi in range(nq) for kj in range(qi + 1)]


def _diff_attn_kernel(lam_ref, q_ref, k_ref, v_ref, g_ref, bias_ref, gain_ref,
                      o_ref, vt_s, q2t_s, acc_s, m_s, s_scr, p_scr, *, out_scale):
    t = DIFF_T
    nq = q_ref.shape[0] // t
    pairs = _diff_pairs(nq)
    n_pairs = len(pairs)
    u = DIFF_UNROLL
    assert n_pairs % u == 0

    half = lax.broadcasted_iota(jnp.int32, (LANES, t), 0) < LANES // 2
    ones = jnp.ones((ONES_ROWS, t), BF16)
    for j in range(nq):
        rows = pl.ds(j * t, t)
        vt_s[j] = jnp.concatenate([v_ref[rows, :].astype(F32).T.astype(BF16), ones], axis=0)
        qt = q_ref[rows, :].astype(F32).T
        q2t_s[j] = jnp.concatenate([jnp.where(half, qt, 0.0), jnp.where(half, 0.0, qt)],
                                   axis=1).astype(BF16)

    n_groups = n_pairs // u

    def scores(grp, slot):
        for c in range(u):
            qi, kj = pairs[grp * u + c]
            s_scr[slot, c] = jnp.dot(k_ref[pl.ds(kj * t, t), :], q2t_s[qi],
                                     preferred_element_type=F32)

    def probs(grp, slot):
        for c in range(u):
            n = grp * u + c
            qi, kj = pairs[n]
            b = bias_ref[qi - kj]
            s = s_scr[slot, c] + jnp.concatenate([b, b], axis=1)
            m_blk = jnp.max(s, axis=0, keepdims=True)
            m_s[n] = m_blk
            p_scr[slot, c] = jnp.exp2(s - m_blk).astype(BF16)

    def values(grp, slot):
        for c in range(u):
            n = grp * u + c
            acc_s[n] = jnp.dot(vt_s[pairs[n][1]], p_scr[slot, c], preferred_element_type=F32)

    for g in range(n_groups + 2):
        if g >= 2:
            values(g - 2, g % 2)
        if 1 <= g <= n_groups:
            probs(g - 1, (g - 1) % 2)
        if g < n_groups:
            scores(g, g % 2)

    lam = lam_ref[0]
    for qi in range(nq):
        first = qi * (qi + 1) // 2
        ms = [m_s[first + j] for j in range(qi + 1)]
        m_all = functools.reduce(jnp.maximum, ms)
        acc = jnp.zeros((LANES + ONES_ROWS, 2 * t), F32)
        for j in range(qi + 1):
            acc = acc + jnp.exp2(ms[j] - m_all) * acc_s[first + j]
        o = acc[:LANES] / acc[LANES:LANES + 1]
        out = o[:, :t] - lam * o[:, t:]
        out = out * lax.rsqrt(jnp.mean(out * out, axis=0, keepdims=True) + EPS)
        out = (out * gain_ref[...] * out_scale).T
        rows = pl.ds(qi * t, t)
        o_ref[rows, :] = (out * _silu(g_ref[rows, :].astype(F32))).astype(BF16)


def _diff_attn(p4, bias, lam, gain, lam_init):
    _, b, s, _ = p4.shape
    nq = s // DIFF_T
    n_pairs = len(_diff_pairs(nq))

    def col(first):
        return pl.BlockSpec((None, None, s, LANES), lambda bi, h: (first + h, bi, 0, 0))

    smem = pl.BlockSpec(memory_space=pltpu.SMEM)
    kern = functools.partial(_diff_attn_kernel, out_scale=1.0 - lam_init)
    return pl.pallas_call(
        kern,
        grid=(b, A_HEADS),
        in_specs=[
            smem,
            col(COL_AQ), col(COL_AK), col(COL_AV), col(COL_AG),
            pl.BlockSpec((nq, None, DIFF_T, DIFF_T), lambda bi, h: (0, h, 0, 0)),
            pl.BlockSpec((LANES, 1), lambda bi, h: (0, 0)),
        ],
        out_specs=pl.BlockSpec((None, s, LANES), lambda bi, h: (bi, 0, h)),
        out_shape=jax.ShapeDtypeStruct((b, s, D_A), BF16),
        scratch_shapes=[
            pltpu.VMEM((nq, LANES + ONES_ROWS, DIFF_T), BF16),
            pltpu.VMEM((nq, LANES, 2 * DIFF_T), BF16),
            pltpu.VMEM((n_pairs, LANES + ONES_ROWS, 2 * DIFF_T), F32),
            pltpu.VMEM((n_pairs, 1, 2 * DIFF_T), F32),
            pltpu.VMEM((2, DIFF_UNROLL, DIFF_T, 2 * DIFF_T), F32),
            pltpu.VMEM((2, DIFF_UNROLL, DIFF_T, 2 * DIFF_T), BF16),
        ],
        compiler_params=_params(("parallel", "parallel")),
        name="diff_attn",
    )(lam, p4, p4, p4, p4, bias, gain)


DIL_UNROLL = 4
STEP = 4


def _dil_variants(seq):
    out = []
    for p, dil in enumerate(DILATIONS):
        per_residue = seq // dil // BLOCK
        for _ in range(dil):
            out += [2 * p + (1 if j == 0 else 0) for j in range(per_residue)]
    return out


def _dil_split(seq, p):
    out = []
    prev_len, length = seq // DILATIONS[p - 1], seq // DILATIONS[p]
    for r_prev in range(DILATIONS[p - 1]):
        for b in range(STEP):
            r = r_prev + DILATIONS[p - 1] * b
            out.append((r * length, pl.ds(r_prev * prev_len + b, length, stride=STEP)))
    return out


def _dil_attn_kernel(q_ref, k_ref, v_ref, g_ref, bias_ref, o_ref,
                     f32_a, f32_b, qa, ka, va, acc_s, den_s, mst, out_s, s_scr, p_scr, *, seq):
    n_pat = len(DILATIONS)
    n_blocks = n_pat * seq // BLOCK
    u = DIL_UNROLL
    assert n_blocks % u == 0

    ka[pl.ds(0, BLOCK), :] = jnp.zeros((BLOCK, LANES), BF16)
    va[pl.ds(0, BLOCK), :] = jnp.zeros((BLOCK, 2 * LANES), BF16)
    va[pl.ds(BLOCK, n_pat * seq), pl.ds(LANES, LANES)] = jnp.ones((n_pat * seq, LANES), BF16)
    for src, dst, front in ((q_ref, qa, 0), (k_ref, ka, BLOCK), (v_ref, va, BLOCK)):
        dst[pl.ds(front, seq), pl.ds(0, LANES)] = src[...]
        cur, nxt = f32_a, f32_b
        cur[...] = src[...].astype(F32)
        for p in range(1, n_pat):
            for row, piece in _dil_split(seq, p):
                x = cur[piece, :]
                if p + 1 < n_pat:
                    nxt[pl.ds(row, seq // DILATIONS[p]), :] = x
                dst[pl.ds(front + p * seq + row, seq // DILATIONS[p]), pl.ds(0, LANES)] = (
                    x.astype(BF16))
            cur, nxt = nxt, cur

    n_groups = n_blocks // u
    variants = _dil_variants(seq)

    def scores(grp, slot):
        for c in range(u):
            base = (grp * u + c) * BLOCK
            s_scr[slot, c] = lax.dot_general(
                qa[pl.ds(base, BLOCK), :], ka[pl.ds(base, 2 * BLOCK), :],
                (((1,), (1,)), ((), ())), preferred_element_type=F32)

    def probs(grp, slot):
        for c in range(u):
            g = grp * u + c
            s = s_scr[slot, c] + bias_ref[variants[g]]
            m = jnp.max(s, axis=1, keepdims=True)
            mst[pl.ds(g * BLOCK, BLOCK), :] = jnp.broadcast_to(m, (BLOCK, LANES))
            p_scr[slot, c] = jnp.exp2(s - m).astype(BF16)

    def values(grp, slot):
        for c in range(u):
            base = (grp * u + c) * BLOCK
            pv = jnp.dot(p_scr[slot, c], va[pl.ds(base, 2 * BLOCK), :],
                         preferred_element_type=F32)
            acc_s[pl.ds(base, BLOCK), :] = pv[:, :LANES]
            den_s[pl.ds(base, BLOCK), :] = pv[:, LANES:]

    for g in range(n_groups + 2):
        if g >= 2:
            values(g - 2, g % 2)
        if 1 <= g <= n_groups:
            probs(g - 1, (g - 1) % 2)
        if g < n_groups:
            scores(g, g % 2)

    for p in range(n_pat - 1, 0, -1):
        length = seq // DILATIONS[p]
        for row, piece in _dil_split(seq, p):
            for k in range(length // BLOCK):
                fine = pl.ds(p * seq + row + k * BLOCK, BLOCK)
                coarse = pl.ds((p - 1) * seq + piece.start + STEP * BLOCK * k, BLOCK, stride=STEP)
                m_a, m_b = mst[coarse, :], mst[fine, :]
                m_ab = jnp.maximum(m_a, m_b)
                w_a, w_b = jnp.exp2(m_a - m_ab), jnp.exp2(m_b - m_ab)
                acc = w_a * acc_s[coarse, :] + w_b * acc_s[fine, :]
                den = w_a * den_s[coarse, :] + w_b * den_s[fine, :]
                if p > 1:
                    mst[coarse, :] = m_ab
                    acc_s[coarse, :] = acc
                    den_s[coarse, :] = den
                else:
                    out_s[coarse, :] = acc / den
    o_ref[...] = (out_s[...] * _silu(g_ref[...].astype(F32))).astype(BF16)


def _dil_attn(p4, bias):
    _, b, s, _ = p4.shape
    assert all(b == a * STEP for a, b in zip(DILATIONS, DILATIONS[1:])) and DILATIONS[0] == 1
    n_pat = len(DILATIONS)

    def col(first):
        return pl.BlockSpec((None, None, s, LANES), lambda bi, h: (first + h, bi, 0, 0))

    rows = n_pat * s
    return pl.pallas_call(
        functools.partial(_dil_attn_kernel, seq=s),
        grid=(b, B_HEADS),
        in_specs=[
            col(COL_BQ), col(COL_BK), col(COL_BV), col(COL_BG),
            pl.BlockSpec((2 * n_pat, None, BLOCK, 2 * BLOCK), lambda bi, h: (0, h, 0, 0)),
        ],
        out_specs=pl.BlockSpec((None, s, LANES), lambda bi, h: (bi, 0, h)),
        out_shape=jax.ShapeDtypeStruct((b, s, D_B), BF16),
        scratch_shapes=[
            pltpu.VMEM((s, LANES), F32),
            pltpu.VMEM((s, LANES), F32),
            pltpu.VMEM((rows, LANES), BF16),
            pltpu.VMEM((BLOCK + rows, LANES), BF16),
            pltpu.VMEM((BLOCK + rows, 2 * LANES), BF16),
            pltpu.VMEM((rows, LANES), F32),
            pltpu.VMEM((rows, LANES), F32),
            pltpu.VMEM((rows, LANES), F32),
            pltpu.VMEM((s, LANES), F32),
            pltpu.VMEM((2, DIL_UNROLL, BLOCK, 2 * BLOCK), F32),
            pltpu.VMEM((2, DIL_UNROLL, BLOCK, 2 * BLOCK), BF16),
        ],
        compiler_params=_params(("parallel", "parallel")),
        name="dil_attn",
    )(p4, p4, p4, p4, bias)


CONV_ROWS = 256


def _conv_kernel(u_ref, glu_ref, g_ref, dw_ref, db_ref, lng_ref, lnb_ref, pw_ref, o_ref,
                 hpad, hc, *, seq):
    for c in range(C_BLOCKS):
        hpad[c, pl.ds(0, CONV_PAD), :] = jnp.zeros((CONV_PAD, LANES), F32)
        u = u_ref[c].astype(F32)
        gate = glu_ref[c].astype(F32)
        hpad[c, pl.ds(CONV_PAD, seq), :] = u * (1.0 / (1.0 + jnp.exp(-gate)))

    first = CONV_PAD - (CONV_WIDTH - 1)
    for c in range(C_BLOCKS):
        cols = slice(c * LANES, (c + 1) * LANES)

        def conv_chunk(ci, carry, c=c, cols=cols):
            r0 = pl.multiple_of(ci * CONV_ROWS, CONV_ROWS)
            acc = jnp.broadcast_to(db_ref[:, cols], (CONV_ROWS, LANES))
            for j in range(CONV_WIDTH):
                tap = hpad[c, pl.ds(r0 + (first + j), CONV_ROWS), :]
                acc = acc + dw_ref[pl.ds(j, 1), cols] * tap
            hc[pl.ds(r0, CONV_ROWS), cols] = acc
            return carry

        lax.fori_loop(0, seq // CONV_ROWS, conv_chunk, 0)

    def tail(ci, carry):
        r0 = pl.multiple_of(ci * CONV_ROWS, CONV_ROWS)
        sl = pl.ds(r0, CONV_ROWS)
        x = hc[sl, :]
        mu = jnp.mean(x, axis=-1, keepdims=True)
        xc = x - mu
        var = jnp.mean(xc * xc, axis=-1, keepdims=True)
        y = xc * lax.rsqrt(var + EPS) * lng_ref[...] + lnb_ref[...]
        y = _silu(y)
        z = jnp.dot(y.astype(BF16), pw_ref[...], preferred_element_type=F32)
        for c in range(C_BLOCKS):
            cols = slice(c * LANES, (c + 1) * LANES)
            o_ref[sl, cols] = (z[:, cols] * _silu(g_ref[c, sl, :].astype(F32))).astype(BF16)
        return carry

    lax.fori_loop(0, seq // CONV_ROWS, tail, 0)


def _conv_mixer(p4, conv_dw, conv_b, ln_g, ln_b, pw_bf16):
    _, b, s, _ = p4.shape

    def cols(first):
        return lambda bi: (first // C_BLOCKS, bi, 0, 0)

    def whole(shape):
        return pl.BlockSpec(shape, lambda bi: (0,) * len(shape))

    blocks = (C_BLOCKS, None, s, LANES)
    return pl.pallas_call(
        functools.partial(_conv_kernel, seq=s),
        grid=(b,),
        in_specs=[
            pl.BlockSpec(blocks, cols(COL_CU)),
            pl.BlockSpec(blocks, cols(COL_CGLU)),
            pl.BlockSpec(blocks, cols(COL_CG)),
            whole((CONV_WIDTH, D_C)),
            whole((1, D_C)),
            whole((1, D_C)),
            whole((1, D_C)),
            whole((D_C, D_C)),
        ],
        out_specs=pl.BlockSpec((None, s, D_C), lambda bi: (bi, 0, 0)),
        out_shape=jax.ShapeDtypeStruct((b, s, D_C), BF16),
        scratch_shapes=[pltpu.VMEM((C_BLOCKS, CONV_PAD + s, LANES), F32),
                        pltpu.VMEM((s, D_C), F32)],
        compiler_params=_params(("parallel",)),
        name="conv_mixer",
    )(p4, p4, p4, conv_dw, conv_b.reshape(1, D_C), ln_g.reshape(1, D_C), ln_b.reshape(1, D_C),
      pw_bf16)


OUT_TM = 512
OUT_ROWS = 256


def _out_proj_kernel(ya_ref, yb_ref, yc_ref, x_ref, w_ref, g_ref, b_ref, o_ref, *, alpha):
    for c in range(OUT_TM // OUT_ROWS):
        rows = pl.ds(c * OUT_ROWS, OUT_ROWS)
        y = jnp.dot(ya_ref[rows, :], w_ref[pl.ds(0, D_A), :], preferred_element_type=F32)
        y = y + jnp.dot(yb_ref[rows, :], w_ref[pl.ds(D_A, D_B), :], preferred_element_type=F32)
        y = y + jnp.dot(yc_ref[rows, :], w_ref[pl.ds(D_A + D_B, D_C), :],
                        preferred_element_type=F32)
        z = alpha * x_ref[rows, :] + y
        mu = jnp.mean(z, axis=-1, keepdims=True)
        zc = z - mu
        var = jnp.mean(zc * zc, axis=-1, keepdims=True)
        o_ref[rows, :] = zc * lax.rsqrt(var + EPS) * g_ref[...] + b_ref[...]


def _out_proj(ya, yb, yc, x2d, w_bf16, ln_g, ln_b, alpha):
    m = x2d.shape[0]

    def rows(width):
        return pl.BlockSpec((OUT_TM, width), lambda i: (i, 0))

    def whole(shape):
        return pl.BlockSpec(shape, lambda i: (0, 0))

    return pl.pallas_call(
        functools.partial(_out_proj_kernel, alpha=alpha),
        grid=(m // OUT_TM,),
        in_specs=[rows(D_A), rows(D_B), rows(D_C), rows(D_MODEL),
                  whole((D_MODEL, D_MODEL)), whole((1, D_MODEL)), whole((1, D_MODEL))],
        out_specs=rows(D_MODEL),
        out_shape=jax.ShapeDtypeStruct((m, D_MODEL), F32),
        compiler_params=_params(("parallel",)),
        name="out_proj",
    )(ya, yb, yc, x2d, w_bf16, ln_g.reshape(1, D_MODEL), ln_b.reshape(1, D_MODEL))


def _in_col_scale():
    scale = jnp.ones((D_IN,), F32)
    scale = scale.at[COL_AQ * LANES:COL_AQ * LANES + D_A].set((LANES // 2) ** -0.5 * LOG2E)
    return scale.at[COL_BQ * LANES:COL_BQ * LANES + D_B].set(LANES ** -0.5 * LOG2E)


def _layer(x2d, batch, seq, layer, depth, w_in, diff_lambda, diff_head_gain, conv_dw, conv_b,
           conv_ln_g, conv_ln_b, conv_pw, w_out, ln_g, ln_b, diff_bias, dil_bias):
    m = batch * seq
    lam_init = 0.8 - 0.6 * math.exp(-0.3 * layer)
    alpha = (2 * depth) ** 0.25
    proj = _in_proj(x2d, (w_in * _in_col_scale()).astype(BF16))
    p4 = proj.reshape(N_COL_BLOCKS, batch, seq, LANES)

    lam_v = diff_lambda.astype(F32)
    lam = (jnp.exp(jnp.sum(lam_v[0] * lam_v[1])) - jnp.exp(jnp.sum(lam_v[2] * lam_v[3]))
           + lam_init).reshape(1)
    ya = _diff_attn(p4, diff_bias, lam, diff_head_gain.astype(F32).reshape(LANES, 1), lam_init)
    yb = _dil_attn(p4, dil_bias)
    yc = _conv_mixer(p4, conv_dw, conv_b, conv_ln_g, conv_ln_b, conv_pw.astype(BF16))
    return _out_proj(ya.reshape(m, D_A), yb.reshape(m, D_B), yc.reshape(m, D_C), x2d,
                     w_out.astype(BF16), ln_g, ln_b, alpha)


def kernel(x, w_in, diff_lambda, diff_head_gain, conv_dw, conv_b, conv_ln_g, conv_ln_b, conv_pw,
           w_out, ln_g, ln_b, rel_bias):
    batch, seq, _ = x.shape
    depth = w_in.shape[0]
    diff_bias = _diff_bias_tiles(rel_bias, seq)
    dil_bias = _dil_bias_tiles(rel_bias)
    x2d = x.reshape(batch * seq, D_MODEL)
    for layer in range(depth):
        x2d = _layer(x2d, batch, seq, layer, depth, w_in[layer], diff_lambda[layer],
                     diff_head_gain[layer], conv_dw[layer], conv_b[layer], conv_ln_g[layer],
                     conv_ln_b[layer], conv_pw[layer], w_out[layer], ln_g[layer], ln_b[layer],
                     diff_bias, dil_bias)
    return x2d.reshape(batch, seq, D_MODEL)
```

```python
import functools
import math

import jax
import jax.numpy as jnp
from jax import lax
from jax.experimental import pallas as pl
from jax.experimental.pallas import tpu as pltpu

F32 = jnp.float32
BF16 = jnp.bfloat16

LANES = 128
D_MODEL = 2048
A_HEADS = 4
B_HEADS = 8
D_A = A_HEADS * LANES
D_B = B_HEADS * LANES
D_C = D_MODEL - D_A - D_B
C_BLOCKS = D_C // LANES
CONV_WIDTH = 31
CONV_PAD = 32
DILATIONS = (1, 4, 16)
BLOCK = 128
N_BUCKETS = 32
MAX_DISTANCE = 2048
EPS = 1e-5
LOG2E = math.log2(math.e)
D_IN = 4 * D_A + 4 * D_B + 3 * D_C
N_COL_BLOCKS = D_IN // LANES
COL_AQ, COL_AK, COL_AV, COL_AG = 0, 4, 8, 12
COL_BQ, COL_BK, COL_BV, COL_BG = 16, 24, 32, 40
COL_CU, COL_CGLU, COL_CG = 48, 52, 56

VMEM_LIMIT_BYTES = 56 * 1024 * 1024


def _params(semantics):
    return pltpu.CompilerParams(dimension_semantics=semantics, vmem_limit_bytes=VMEM_LIMIT_BYTES)


def _silu(x):
    return x * (1.0 / (1.0 + jnp.exp(-x)))


IN_TM = 1024
IN_TN = 1280


def _in_proj_kernel(x_ref, w_ref, o_ref, xb_ref):
    @pl.when(pl.program_id(1) == 0)
    def _():
        xb_ref[...] = x_ref[...].astype(BF16)

    acc = jnp.dot(xb_ref[...], w_ref[...], preferred_element_type=F32)
    for c in range(IN_TN // LANES):
        o_ref[c] = acc[:, c * LANES:(c + 1) * LANES].astype(BF16)


def _in_proj(x2d, w_bf16):
    m = x2d.shape[0]
    nb = IN_TN // LANES
    return pl.pallas_call(
        _in_proj_kernel,
        grid=(m // IN_TM, D_IN // IN_TN),
        in_specs=[
            pl.BlockSpec((IN_TM, D_MODEL), lambda i, j: (i, 0)),
            pl.BlockSpec((D_MODEL, IN_TN), lambda i, j: (0, j)),
        ],
        out_specs=pl.BlockSpec((nb, IN_TM, LANES), lambda i, j: (j, i, 0)),
        out_shape=jax.ShapeDtypeStruct((N_COL_BLOCKS, m, LANES), BF16),
        scratch_shapes=[pltpu.VMEM((IN_TM, D_MODEL), BF16)],
        compiler_params=_params(("parallel", "arbitrary")),
        name="in_proj",
    )(x2d, w_bf16)


def _t5_bucket(dist):
    max_exact = N_BUCKETS // 2
    d = jnp.maximum(dist, 0)
    df = jnp.maximum(d, 1).astype(F32)
    large = max_exact + (jnp.log(df / max_exact) / math.log(MAX_DISTANCE / max_exact)
                         * (N_BUCKETS - max_exact)).astype(jnp.int32)
    large = jnp.minimum(large, N_BUCKETS - 1)
    return jnp.where(d < max_exact, d, large)


DIFF_T = 256


def _bucket_lookup(table, bucket):
    onehot = bucket[..., None] == jnp.arange(N_BUCKETS)
    cols = [jnp.sum(jnp.where(onehot, table[:, h].astype(F32), 0.0), axis=-1)
            for h in range(table.shape[1])]
    return jnp.stack(cols)


def _diff_bias_tiles(rel_bias, seq):
    t = DIFF_T
    width = seq + t
    dist = jnp.arange(width) - (t - 1)
    line = _bucket_lookup(rel_bias[:, :A_HEADS], _t5_bucket(jnp.clip(dist, 0, seq - 1)))
    line = jnp.where(dist[None] >= 0, line * LOG2E, -jnp.inf)
    skew = jnp.tile(line, (1, t))[:, :t * (width - 1)].reshape(A_HEADS, t, width - 1)
    by_key = skew[:, :, t - 1:t - 1 + seq]
    return by_key.reshape(A_HEADS, t, seq // t, t).transpose(2, 0, 1, 3)


def _dil_bias_tiles(rel_bias):
    qi = jnp.arange(BLOCK)[:, None]
    kj = jnp.arange(2 * BLOCK)[None, :]
    lag = BLOCK + qi - kj
    valid = (lag >= 0) & (lag <= BLOCK)
    tiles = []
    for dil in DILATIONS:
        b = _bucket_lookup(rel_bias[:, A_HEADS:],
                           _t5_bucket(jnp.clip(lag, 0, 2 * BLOCK - 1) * dil))
        tiles.append(jnp.where(valid[None], b * LOG2E, -jnp.inf))
        tiles.append(jnp.where((valid & (kj >= BLOCK))[None], b * LOG2E, -jnp.inf))
    return jnp.stack(tiles)


DIFF_UNROLL = 3
ONES_ROWS = 16


def _diff_pairs(nq):
    return [(qi, kj) for qi in range(nq) for kj in range(qi + 1)]


def _diff_attn_kernel(lam_ref, q_ref, k_ref, v_ref, g_ref, bias_ref, gain_ref,
                      o_ref, vt_s, q2t_s, acc_s, m_s, s_scr, p_scr, *, out_scale):
    t = DIFF_T
    nq = q_ref.shape[0] // t
    pairs = _diff_pairs(nq)
    n_pairs = len(pairs)
    u = DIFF_UNROLL
    assert n_pairs % u == 0

    half = lax.broadcasted_iota(jnp.int32, (LANES, t), 0) < LANES // 2
    ones = jnp.ones((ONES_ROWS, t), BF16)
    for j in range(nq):
        rows = pl.ds(j * t, t)
        vt_s[j] = jnp.concatenate([v_ref[rows, :].astype(F32).T.astype(BF16), ones], axis=0)
        qt = q_ref[rows, :].astype(F32).T
        q2t_s[j] = jnp.concatenate([jnp.where(half, qt, 0.0), jnp.where(half, 0.0, qt)],
                                   axis=1).astype(BF16)

    n_groups = n_pairs // u

    def scores(grp, slot):
        for c in range(u):
            qi, kj = pairs[grp * u + c]
            s_scr[slot, c] = jnp.dot(k_ref[pl.ds(kj * t, t), :], q2t_s[qi],
                                     preferred_element_type=F32)

    def probs(grp, slot):
        for c in range(u):
            n = grp * u + c
            qi, kj = pairs[n]
            b = bias_ref[qi - kj]
            s = s_scr[slot, c] + jnp.concatenate([b, b], axis=1)
            m_blk = jnp.max(s, axis=0, keepdims=True)
            m_s[n] = m_blk
            p_scr[slot, c] = jnp.exp2(s - m_blk).astype(BF16)

    def values(grp, slot):
        for c in range(u):
            n = grp * u + c
            acc_s[n] = jnp.dot(vt_s[pairs[n][1]], p_scr[slot, c], preferred_element_type=F32)

    for g in range(n_groups + 2):
        if g >= 2:
            values(g - 2, g % 2)
        if 1 <= g <= n_groups:
            probs(g - 1, (g - 1) % 2)
        if g < n_groups:
            scores(g, g % 2)

    lam = lam_ref[0]
    for qi in range(nq):
        first = qi * (qi + 1) // 2
        ms = [m_s[first + j] for j in range(qi + 1)]
        m_all = functools.reduce(jnp.maximum, ms)
        acc = jnp.zeros((LANES + ONES_ROWS, 2 * t), F32)
        for j in range(qi + 1):
            acc = acc + jnp.exp2(ms[j] - m_all) * acc_s[first + j]
        o = acc[:LANES] / acc[LANES:LANES + 1]
        out = o[:, :t] - lam * o[:, t:]
        out = out * lax.rsqrt(jnp.mean(out * out, axis=0, keepdims=True) + EPS)
        out = (out * gain_ref[...] * out_scale).T
        rows = pl.ds(qi * t, t)
        o_ref[rows, :] = (out * _silu(g_ref[rows, :].astype(F32))).astype(BF16)


def _diff_attn(p4, bias, lam, gain, lam_init):
    _, b, s, _ = p4.shape
    nq = s // DIFF_T
    n_pairs = len(_diff_pairs(nq))

    def col(first):
        return pl.BlockSpec((None, None, s, LANES), lambda bi, h: (first + h, bi, 0, 0))

    smem = pl.BlockSpec(memory_space=pltpu.SMEM)
    kern = functools.partial(_diff_attn_kernel, out_scale=1.0 - lam_init)
    return pl.pallas_call(
        kern,
        grid=(b, A_HEADS),
        in_specs=[
            smem,
            col(COL_AQ), col(COL_AK), col(COL_AV), col(COL_AG),
            pl.BlockSpec((nq, None, DIFF_T, DIFF_T), lambda bi, h: (0, h, 0, 0)),
            pl.BlockSpec((LANES, 1), lambda bi, h: (0, 0)),
        ],
        out_specs=pl.BlockSpec((None, s, LANES), lambda bi, h: (bi, 0, h)),
        out_shape=jax.ShapeDtypeStruct((b, s, D_A), BF16),
        scratch_shapes=[
            pltpu.VMEM((nq, LANES + ONES_ROWS, DIFF_T), BF16),
            pltpu.VMEM((nq, LANES, 2 * DIFF_T), BF16),
            pltpu.VMEM((n_pairs, LANES + ONES_ROWS, 2 * DIFF_T), F32),
            pltpu.VMEM((n_pairs, 1, 2 * DIFF_T), F32),
            pltpu.VMEM((2, DIFF_UNROLL, DIFF_T, 2 * DIFF_T), F32),
            pltpu.VMEM((2, DIFF_UNROLL, DIFF_T, 2 * DIFF_T), BF16),
        ],
        compiler_params=_params(("parallel", "parallel")),
        name="diff_attn",
    )(lam, p4, p4, p4, p4, bias, gain)


DIL_UNROLL = 4
STEP = 4


def _dil_variants(seq):
    out = []
    for p, dil in enumerate(DILATIONS):
        per_residue = seq // dil // BLOCK
        for _ in range(dil):
            out += [2 * p + (1 if j == 0 else 0) for j in range(per_residue)]
    return out


def _dil_split(seq, p):
    out = []
    prev_len, length = seq // DILATIONS[p - 1], seq // DILATIONS[p]
    for r_prev in range(DILATIONS[p - 1]):
        for b in range(STEP):
            r = r_prev + DILATIONS[p - 1] * b
            out.append((r * length, pl.ds(r_prev * prev_len + b, length, stride=STEP)))
    return out


def _dil_attn_kernel(q_ref, k_ref, v_ref, g_ref, bias_ref, o_ref,
                     f32_a, f32_b, qa, ka, va, acc_s, den_s, mst, out_s, s_scr, p_scr, *, seq):
    n_pat = len(DILATIONS)
    n_blocks = n_pat * seq // BLOCK
    u = DIL_UNROLL
    assert n_blocks % u == 0

    ka[pl.ds(0, BLOCK), :] = jnp.zeros((BLOCK, LANES), BF16)
    va[pl.ds(0, BLOCK), :] = jnp.zeros((BLOCK, 2 * LANES), BF16)
    va[pl.ds(BLOCK, n_pat * seq), pl.ds(LANES, LANES)] = jnp.ones((n_pat * seq, LANES), BF16)
    for src, dst, front in ((q_ref, qa, 0), (k_ref, ka, BLOCK), (v_ref, va, BLOCK)):
        dst[pl.ds(front, seq), pl.ds(0, LANES)] = src[...]
        cur, nxt = f32_a, f32_b
        cur[...] = src[...].astype(F32)
        for p in range(1, n_pat):
            for row, piece in _dil_split(seq, p):
                x = cur[piece, :]
                if p + 1 < n_pat:
                    nxt[pl.ds(row, seq // DILATIONS[p]), :] = x
                dst[pl.ds(front + p * seq + row, seq // DILATIONS[p]), pl.ds(0, LANES)] = (
                    x.astype(BF16))
            cur, nxt = nxt, cur

    n_groups = n_blocks // u
    variants = _dil_variants(seq)

    def scores(grp, slot):
        for c in range(u):
            g = grp * u + c
            s = lax.dot_general(qa[pl.ds(g * BLOCK, BLOCK), :],
                                ka[pl.ds(g * BLOCK, 2 * BLOCK), :],
                                (((1,), (1,)), ((), ())), preferred_element_type=F32)
            s_scr[slot, c] = s + bias_ref[variants[g]]

    def maxes(grp, slot):
        for c in range(u):
            m = jnp.max(s_scr[slot, c], axis=1, keepdims=True)
            mst[pl.ds((grp * u + c) * BLOCK, BLOCK), :] = jnp.broadcast_to(m, (BLOCK, LANES))

    def probs(grp, s_slot, p_slot):
        for c in range(u):
            m = mst[pl.ds((grp * u + c) * BLOCK, BLOCK), :]
            s = s_scr[s_slot, c] - jnp.concatenate([m, m], axis=1)
            p_scr[p_slot, c] = jnp.exp2(s).astype(BF16)

    def values(grp, slot):
        for c in range(u):
            base = (grp * u + c) * BLOCK
            pv = jnp.dot(p_scr[slot, c], va[pl.ds(base, 2 * BLOCK), :],
                         preferred_element_type=F32)
            acc_s[pl.ds(base, BLOCK), :] = pv[:, :LANES]
            den_s[pl.ds(base, BLOCK), :] = pv[:, LANES:]

    for g in range(n_groups + 3):
        if g >= 3:
            values(g - 3, (g - 3) % 2)
        if 2 <= g < n_groups + 2:
            probs(g - 2, (g - 2) % 3, (g - 2) % 2)
        if 1 <= g < n_groups + 1:
            maxes(g - 1, (g - 1) % 3)
        if g < n_groups:
            scores(g, g % 3)

    for p in range(n_pat - 1, 0, -1):
        length = seq // DILATIONS[p]
        for row, piece in _dil_split(seq, p):
            for k in range(length // BLOCK):
                fine = pl.ds(p * seq + row + k * BLOCK, BLOCK)
                coarse = pl.ds((p - 1) * seq + piece.start + STEP * BLOCK * k, BLOCK, stride=STEP)
                m_a, m_b = mst[coarse, :], mst[fine, :]
                m_ab = jnp.maximum(m_a, m_b)
                w_a, w_b = jnp.exp2(m_a - m_ab), jnp.exp2(m_b - m_ab)
                acc = w_a * acc_s[coarse, :] + w_b * acc_s[fine, :]
                den = w_a * den_s[coarse, :] + w_b * den_s[fine, :]
                if p > 1:
                    mst[coarse, :] = m_ab
                    acc_s[coarse, :] = acc
                    den_s[coarse, :] = den
                else:
                    out_s[coarse, :] = acc / den
    o_ref[...] = (out_s[...] * _silu(g_ref[...].astype(F32))).astype(BF16)


def _dil_attn(p4, bias):
    _, b, s, _ = p4.shape
    assert all(b == a * STEP for a, b in zip(DILATIONS, DILATIONS[1:])) and DILATIONS[0] == 1
    n_pat = len(DILATIONS)

    def col(first):
        return pl.BlockSpec((None, None, s, LANES), lambda bi, h: (first + h, bi, 0, 0))

    rows = n_pat * s
    return pl.pallas_call(
        functools.partial(_dil_attn_kernel, seq=s),
        grid=(b, B_HEADS),
        in_specs=[
            col(COL_BQ), col(COL_BK), col(COL_BV), col(COL_BG),
            pl.BlockSpec((2 * n_pat, None, BLOCK, 2 * BLOCK), lambda bi, h: (0, h, 0, 0)),
        ],
        out_specs=pl.BlockSpec((None, s, LANES), lambda bi, h: (bi, 0, h)),
        out_shape=jax.ShapeDtypeStruct((b, s, D_B), BF16),
        scratch_shapes=[
            pltpu.VMEM((s, LANES), F32),
            pltpu.VMEM((s, LANES), F32),
            pltpu.VMEM((rows, LANES), BF16),
            pltpu.VMEM((BLOCK + rows, LANES), BF16),
            pltpu.VMEM((BLOCK + rows, 2 * LANES), BF16),
            pltpu.VMEM((rows, LANES), F32),
            pltpu.VMEM((rows, LANES), F32),
            pltpu.VMEM((rows, LANES), F32),
            pltpu.VMEM((s, LANES), F32),
            pltpu.VMEM((3, DIL_UNROLL, BLOCK, 2 * BLOCK), F32),
            pltpu.VMEM((2, DIL_UNROLL, BLOCK, 2 * BLOCK), BF16),
        ],
        compiler_params=_params(("parallel", "parallel")),
        name="dil_attn",
    )(p4, p4, p4, p4, bias)


CONV_ROWS = 256


def _conv_kernel(u_ref, glu_ref, g_ref, dw_ref, db_ref, lng_ref, lnb_ref, pw_ref, o_ref,
                 hpad, hc, *, seq):
    for c in range(C_BLOCKS):
        hpad[c, pl.ds(0, CONV_PAD), :] = jnp.zeros((CONV_PAD, LANES), F32)
        u = u_ref[c].astype(F32)
        gate = glu_ref[c].astype(F32)
        hpad[c, pl.ds(CONV_PAD, seq), :] = u * (1.0 / (1.0 + jnp.exp(-gate)))

    first = CONV_PAD - (CONV_WIDTH - 1)
    for c in range(C_BLOCKS):
        cols = slice(c * LANES, (c + 1) * LANES)

        def conv_chunk(ci, carry, c=c, cols=cols):
            r0 = pl.multiple_of(ci * CONV_ROWS, CONV_ROWS)
            acc = jnp.broadcast_to(db_ref[:, cols], (CONV_ROWS, LANES))
            for j in range(CONV_WIDTH):
                tap = hpad[c, pl.ds(r0 + (first + j), CONV_ROWS), :]
                acc = acc + dw_ref[pl.ds(j, 1), cols] * tap
            hc[pl.ds(r0, CONV_ROWS), cols] = acc
            return carry

        lax.fori_loop(0, seq // CONV_ROWS, conv_chunk, 0)

    def tail(ci, carry):
        r0 = pl.multiple_of(ci * CONV_ROWS, CONV_ROWS)
        sl = pl.ds(r0, CONV_ROWS)
        x = hc[sl, :]
        mu = jnp.mean(x, axis=-1, keepdims=True)
        xc = x - mu
        var = jnp.mean(xc * xc, axis=-1, keepdims=True)
        y = xc * lax.rsqrt(var + EPS) * lng_ref[...] + lnb_ref[...]
        y = _silu(y)
        z = jnp.dot(y.astype(BF16), pw_ref[...], preferred_element_type=F32)
        for c in range(C_BLOCKS):
            cols = slice(c * LANES, (c + 1) * LANES)
            o_ref[sl, cols] = (z[:, cols] * _silu(g_ref[c, sl, :].astype(F32))).astype(BF16)
        return carry

    lax.fori_loop(0, seq // CONV_ROWS, tail, 0)


def _conv_mixer(p4, conv_dw, conv_b, ln_g, ln_b, pw_bf16):
    _, b, s, _ = p4.shape

    def cols(first):
        return lambda bi: (first // C_BLOCKS, bi, 0, 0)

    def whole(shape):
        return pl.BlockSpec(shape, lambda bi: (0,) * len(shape))

    blocks = (C_BLOCKS, None, s, LANES)
    return pl.pallas_call(
        functools.partial(_conv_kernel, seq=s),
        grid=(b,),
        in_specs=[
            pl.BlockSpec(blocks, cols(COL_CU)),
            pl.BlockSpec(blocks, cols(COL_CGLU)),
            pl.BlockSpec(blocks, cols(COL_CG)),
            whole((CONV_WIDTH, D_C)),
            whole((1, D_C)),
            whole((1, D_C)),
            whole((1, D_C)),
            whole((D_C, D_C)),
        ],
        out_specs=pl.BlockSpec((None, s, D_C), lambda bi: (bi, 0, 0)),
        out_shape=jax.ShapeDtypeStruct((b, s, D_C), BF16),
        scratch_shapes=[pltpu.VMEM((C_BLOCKS, CONV_PAD + s, LANES), F32),
                        pltpu.VMEM((s, D_C), F32)],
        compiler_params=_params(("parallel",)),
        name="conv_mixer",
    )(p4, p4, p4, conv_dw, conv_b.reshape(1, D_C), ln_g.reshape(1, D_C), ln_b.reshape(1, D_C),
      pw_bf16)


OUT_TM = 512
OUT_ROWS = 256


def _out_proj_kernel(ya_ref, yb_ref, yc_ref, x_ref, w_ref, g_ref, b_ref, o_ref, *, alpha):
    for c in range(OUT_TM // OUT_ROWS):
        rows = pl.ds(c * OUT_ROWS, OUT_ROWS)
        y = jnp.dot(ya_ref[rows, :], w_ref[pl.ds(0, D_A), :], preferred_element_type=F32)
        y = y + jnp.dot(yb_ref[rows, :], w_ref[pl.ds(D_A, D_B), :], preferred_element_type=F32)
        y = y + jnp.dot(yc_ref[rows, :], w_ref[pl.ds(D_A + D_B, D_C), :],
                        preferred_element_type=F32)
        z = alpha * x_ref[rows, :] + y
        mu = jnp.mean(z, axis=-1, keepdims=True)
        zc = z - mu
        var = jnp.mean(zc * zc, axis=-1, keepdims=True)
        o_ref[rows, :] = zc * lax.rsqrt(var + EPS) * g_ref[...] + b_ref[...]


def _out_proj(ya, yb, yc, x2d, w_bf16, ln_g, ln_b, alpha):
    m = x2d.shape[0]

    def rows(width):
        return pl.BlockSpec((OUT_TM, width), lambda i: (i, 0))

    def whole(shape):
        return pl.BlockSpec(shape, lambda i: (0, 0))

    return pl.pallas_call(
        functools.partial(_out_proj_kernel, alpha=alpha),
        grid=(m // OUT_TM,),
        in_specs=[rows(D_A), rows(D_B), rows(D_C), rows(D_MODEL),
                  whole((D_MODEL, D_MODEL)), whole((1, D_MODEL)), whole((1, D_MODEL))],
        out_specs=rows(D_MODEL),
        out_shape=jax.ShapeDtypeStruct((m, D_MODEL), F32),
        compiler_params=_params(("parallel",)),
        name="out_proj",
    )(ya, yb, yc, x2d, w_bf16, ln_g.reshape(1, D_MODEL), ln_b.reshape(1, D_MODEL))


def _in_col_scale():
    scale = jnp.ones((D_IN,), F32)
    scale = scale.at[COL_AQ * LANES:COL_AQ * LANES + D_A].set((LANES // 2) ** -0.5 * LOG2E)
    return scale.at[COL_BQ * LANES:COL_BQ * LANES + D_B].set(LANES ** -0.5 * LOG2E)


def _layer(x2d, batch, seq, layer, depth, w_in, diff_lambda, diff_head_gain, conv_dw, conv_b,
           conv_ln_g, conv_ln_b, conv_pw, w_out, ln_g, ln_b, diff_bias, dil_bias):
    m = batch * seq
    lam_init = 0.8 - 0.6 * math.exp(-0.3 * layer)
    alpha = (2 * depth) ** 0.25
    proj = _in_proj(x2d, (w_in * _in_col_scale()).astype(BF16))
    p4 = proj.reshape(N_COL_BLOCKS, batch, seq, LANES)

    lam_v = diff_lambda.astype(F32)
    lam = (jnp.exp(jnp.sum(lam_v[0] * lam_v[1])) - jnp.exp(jnp.sum(lam_v[2] * lam_v[3]))
           + lam_init).reshape(1)
    ya = _diff_attn(p4, diff_bias, lam, diff_head_gain.astype(F32).reshape(LANES, 1), lam_init)
    yb = _dil_attn(p4, dil_bias)
    yc = _conv_mixer(p4, conv_dw, conv_b, conv_ln_g, conv_ln_b, conv_pw.astype(BF16))
    return _out_proj(ya.reshape(m, D_A), yb.reshape(m, D_B), yc.reshape(m, D_C), x2d,
                     w_out.astype(BF16), ln_g, ln_b, alpha)


def kernel(x, w_in, diff_lambda, diff_head_gain, conv_dw, conv_b, conv_ln_g, conv_ln_b, conv_pw,
           w_out, ln_g, ln_b, rel_bias):
    batch, seq, _ = x.shape
    depth = w_in.shape[0]
    diff_bias = _diff_bias_tiles(rel_bias, seq)
    dil_bias = _dil_bias_tiles(rel_bias)
    x2d = x.reshape(batch * seq, D_MODEL)
    for layer in range(depth):
        x2d = _layer(x2d, batch, seq, layer, depth, w_in[layer], diff_lambda[layer],
                     diff_head_gain[layer], conv_dw[layer], conv_b[layer], conv_ln_g[layer],
                     conv_ln_b[layer], conv_pw[layer], w_out[layer], ln_g[layer], ln_b[layer],
                     diff_bias, dil_bias)
    return x2d.reshape(batch, seq, D_MODEL)
```
